```python
import math
import jax, jax.numpy as jnp
from jax import lax
import numpy as np

D_MODEL = 2048
BATCH = 2
SEQ = 8192
DEPTH = 4

D_MIX = D_MODEL
POOL_WIDTH = D_MIX // 4
POOL_WINDOWS = (2, 4, 8, 16)
POOL_GROUP = POOL_WIDTH // len(POOL_WINDOWS)
SSD_WIDTH = D_MIX // 2
SSD_HEAD_DIM = 64
SSD_HEADS = SSD_WIDTH // SSD_HEAD_DIM
SSD_GROUPS = 2
SSD_STATE = 128
SSD_CONV = 4
SSD_CHUNK = 256
ATTN_WIDTH = D_MIX - POOL_WIDTH - SSD_WIDTH
ATTN_HEAD_DIM = 64
ATTN_HEADS = ATTN_WIDTH // ATTN_HEAD_DIM
ATTN_BLOCK = 128
XBC_WIDTH = SSD_WIDTH + 2 * SSD_GROUPS * SSD_STATE
IN_WIDTH = POOL_WIDTH + SSD_WIDTH + XBC_WIDTH + SSD_HEADS + 3 * ATTN_WIDTH
D_FF = ((8 * D_MODEL // 3 + 255) // 256) * 256
RMS_EPS = 1e-6

kernel_name = "hymba_pool_ssd_stickbreak_macaron"

F32 = jnp.float32


def rms_norm(x, g):
    xf = x.astype(F32)
    y = xf * lax.rsqrt(jnp.mean(xf * xf, axis=-1, keepdims=True) + RMS_EPS)
    return (y * g.astype(F32)).astype(x.dtype)


def swiglu(u, w_gate, w_up, w_down):
    a = jnp.einsum('bsd,df->bsf', u, w_gate)
    b = jnp.einsum('bsd,df->bsf', u, w_up)
    return jnp.einsum('bsf,fd->bsd', jax.nn.silu(a) * b, w_down)


def multiscale_pool(v, w_pool, scale):
    s_len = v.shape[1]
    vf = v.astype(F32)
    cs = jnp.cumsum(vf, axis=1)
    pos = jnp.arange(1, s_len + 1, dtype=F32)
    outs = []
    for i, w in enumerate(POOL_WINDOWS):
        sl = slice(i * POOL_GROUP, (i + 1) * POOL_GROUP)
        c = cs[..., sl]
        prev = jnp.pad(c, ((0, 0), (w, 0), (0, 0)))[:, :s_len]
        mean = (c - prev) / jnp.minimum(pos, float(w))[None, :, None]
        outs.append(jnp.einsum('bsc,cd->bsd', mean - vf[..., sl], w_pool[i].astype(F32)))
    return (jnp.concatenate(outs, axis=-1) * scale.astype(F32)).astype(v.dtype)


def causal_dwconv(x, w, b):
    k_len = w.shape[0]
    s_len = x.shape[1]
    xp = jnp.pad(x, ((0, 0), (k_len - 1, 0), (0, 0)))
    y = xp[:, 0:s_len] * w[0]
    for k in range(1, k_len):
        y = y + xp[:, k:k + s_len] * w[k]
    return y + b


def segsum_exp(a):
    t = a.shape[-1]
    strict = jnp.tril(jnp.ones((t, t), dtype=bool), -1)
    incl = jnp.tril(jnp.ones((t, t), dtype=bool))
    rep = jnp.where(strict, a[..., :, None], 0.0)
    ss = jnp.cumsum(rep, axis=-2)
    return jnp.where(incl, jnp.exp(ss), 0.0)


def ssd_chunked(X, A, Bm, Cm):
    b, s_len, h, p = X.shape
    pad = (-s_len) % SSD_CHUNK
    if pad:
        X = jnp.pad(X, ((0, 0), (0, pad), (0, 0), (0, 0)))
        A = jnp.pad(A, ((0, 0), (0, pad), (0, 0)))
        Bm = jnp.pad(Bm, ((0, 0), (0, pad), (0, 0), (0, 0)))
        Cm = jnp.pad(Cm, ((0, 0), (0, pad), (0, 0), (0, 0)))
    t_len = s_len + pad
    nc, L, g = t_len // SSD_CHUNK, SSD_CHUNK, SSD_GROUPS
    e = h // g
    n = Bm.shape[-1]
    X = X.reshape(b, nc, L, g, e, p)
    A = A.reshape(b, nc, L, g, e).transpose(0, 3, 4, 1, 2)
    Bm = Bm.reshape(b, nc, L, g, n)
    Cm = Cm.reshape(b, nc, L, g, n)
    A_cs = jnp.cumsum(A, axis=-1)
    decay_in = segsum_exp(A)
    CB = jnp.einsum('bclgn,bcsgn->bgcls', Cm, Bm)
    Y_diag = jnp.einsum('bgcls,bgecls,bcsgep->bclgep', CB, decay_in, X)
    decay_states = jnp.exp(A_cs[..., -1:] - A_cs)
    states = jnp.einsum('bclgn,bgecl,bclgep->bcgepn', Bm, decay_states, X)
    chunk_decay = jnp.moveaxis(jnp.exp(A_cs[..., -1]), -1, 0)
    states_c = jnp.moveaxis(states, 1, 0)

    def step(carry, inp):
        s_c, d_c = inp
        return carry * d_c[..., None, None] + s_c, carry

    init = jnp.zeros(states_c.shape[1:], dtype=states_c.dtype)
    _, states_in = lax.scan(step, init, (states_c, chunk_decay))
    Y_off = jnp.einsum('bclgn,cbgepn,bgecl->bclgep', Cm, states_in, jnp.exp(A_cs))
    Y = (Y_diag + Y_off).reshape(b, t_len, h, p)
    return Y[:, :s_len]


def ssd_mixer(z, xbc, dt_raw, conv_w, conv_b, dt_bias, a_log, d_skip, norm_g):
    bsz, s_len, _ = z.shape
    xbc = jax.nn.silu(causal_dwconv(xbc, conv_w, conv_b)).astype(F32)
    gn = SSD_GROUPS * SSD_STATE
    xs, Bm, Cm = jnp.split(xbc, [SSD_WIDTH, SSD_WIDTH + gn], axis=-1)
    xs = xs.reshape(bsz, s_len, SSD_HEADS, SSD_HEAD_DIM)
    Bm = Bm.reshape(bsz, s_len, SSD_GROUPS, SSD_STATE)
    Cm = Cm.reshape(bsz, s_len, SSD_GROUPS, SSD_STATE)
    dt = jax.nn.softplus(dt_raw.astype(F32) + dt_bias.astype(F32))
    A = -jnp.exp(a_log.astype(F32))
    y = ssd_chunked(xs * dt[..., None], A * dt, Bm, Cm)
    y = y + d_skip.astype(F32)[:, None] * xs
    y = y.reshape(bsz, s_len, SSD_WIDTH) * jax.nn.silu(z.astype(F32))
    yg = y.reshape(bsz, s_len, SSD_GROUPS, SSD_WIDTH // SSD_GROUPS)
    yg = yg * lax.rsqrt(jnp.mean(yg * yg, axis=-1, keepdims=True) + RMS_EPS)
    y = yg.reshape(bsz, s_len, SSD_WIDTH) * norm_g.astype(F32)
    return y.astype(z.dtype)


def stick_breaking_attention(q, k, v):
    bsz, s_len, nh, dh = q.shape
    nb = s_len // ATTN_BLOCK
    qf = (q.astype(F32) * (dh ** -0.5)).reshape(bsz, nb, ATTN_BLOCK, nh, dh).transpose(1, 0, 3, 2, 4)
    kf = k.astype(F32).transpose(0, 2, 1, 3)
    vf = v.astype(F32).transpose(0, 2, 1, 3)
    key_pos = jnp.arange(s_len, dtype=jnp.int32)
    starts = jnp.arange(nb, dtype=jnp.int32) * ATTN_BLOCK
    q_off = jnp.arange(ATTN_BLOCK, dtype=jnp.int32)

    def one_block(args):
        qb, q0 = args
        logits = jnp.einsum('bhqd,bhkd->bhqk', qb, kf)
        mask = key_pos[None, :] < (q0 + q_off)[:, None]
        log_1m = jnp.where(mask, jax.nn.log_sigmoid(-logits), 0.0)
        suffix = lax.cumsum(log_1m, axis=3, reverse=True) - log_1m
        weights = jnp.where(mask, jnp.exp(jax.nn.log_sigmoid(logits) + suffix), 0.0)
        return jnp.einsum('bhqk,bhkd->bhqd', weights, vf)

    out = lax.map(one_block, (qf, starts))
    return out.transpose(1, 0, 3, 2, 4).reshape(bsz, s_len, nh * dh).astype(q.dtype)


def hybrid_mixer(u, w_in, pool_w, pool_scale, conv_w, conv_b, dt_bias, a_log, d_skip, ssd_norm, w_out):
    bsz, s_len, _ = u.shape
    proj = jnp.einsum('bsd,dn->bsn', u, w_in)
    idx = np.cumsum([POOL_WIDTH, SSD_WIDTH, XBC_WIDTH, SSD_HEADS, ATTN_WIDTH, ATTN_WIDTH]).tolist()
    pool_in, z, xbc, dt_raw, q, k, v = jnp.split(proj, idx, axis=-1)
    pool_out = multiscale_pool(pool_in, pool_w, pool_scale)
    ssd_out = ssd_mixer(z, xbc, dt_raw, conv_w, conv_b, dt_bias, a_log, d_skip, ssd_norm)
    shp = (bsz, s_len, ATTN_HEADS, ATTN_HEAD_DIM)
    attn_out = stick_breaking_attention(q.reshape(shp), k.reshape(shp), v.reshape(shp))
    mixed = jnp.concatenate([pool_out, ssd_out, attn_out], axis=-1)
    return jnp.einsum('bsm,md->bsd', mixed, w_out)


def setup_inputs(seed: int = 0) -> dict:
    key = jax.random.key(seed)
    ks = jax.random.split(key, 24)

    def nrm(k, shape, scale):
        return jax.random.normal(k, shape, dtype=F32) * scale

    def gain(k, shape):
        return 1.0 + 0.02 * jax.random.normal(k, shape, dtype=F32)

    dt0 = jnp.exp(jax.random.uniform(ks[13], (DEPTH, SSD_HEADS), dtype=F32)
                  * (math.log(0.1) - math.log(0.001)) + math.log(0.001))
    return {
        "x": jax.random.normal(ks[0], (BATCH, SEQ, D_MODEL), dtype=F32),
        "ffn1_norm": gain(ks[1], (DEPTH, D_MODEL)),
        "ffn1_w_gate": nrm(ks[2], (DEPTH, D_MODEL, D_FF), D_MODEL ** -0.5),
        "ffn1_w_up": nrm(ks[3], (DEPTH, D_MODEL, D_FF), D_MODEL ** -0.5),
        "ffn1_w_down": nrm(ks[4], (DEPTH, D_FF, D_MODEL), D_FF ** -0.5),
        "mix_norm": gain(ks[5], (DEPTH, D_MODEL)),
        "w_in": nrm(ks[6], (DEPTH, D_MODEL, IN_WIDTH), D_MODEL ** -0.5),
        "pool_w": nrm(ks[7], (DEPTH, len(POOL_WINDOWS), POOL_GROUP, POOL_GROUP), POOL_GROUP ** -0.5),
        "pool_scale": gain(ks[8], (DEPTH, POOL_WIDTH)),
        "conv_w": nrm(ks[9], (DEPTH, SSD_CONV, XBC_WIDTH), SSD_CONV ** -0.5),
        "conv_b": nrm(ks[10], (DEPTH, XBC_WIDTH), 0.01),
        "dt_bias": dt0 + jnp.log(-jnp.expm1(-dt0)),
        "a_log": jnp.log(jax.random.uniform(ks[11], (DEPTH, SSD_HEADS), dtype=F32, minval=1.0, maxval=16.0)),
        "d_skip": 1.0 + 0.1 * jax.random.normal(ks[12], (DEPTH, SSD_HEADS), dtype=F32),
        "ssd_norm": gain(ks[14], (DEPTH, SSD_WIDTH)),
        "w_out": nrm(ks[15], (DEPTH, D_MIX, D_MODEL), D_MIX ** -0.5),
        "ffn2_norm": gain(ks[16], (DEPTH, D_MODEL)),
        "ffn2_w_gate": nrm(ks[17], (DEPTH, D_MODEL, D_FF), D_MODEL ** -0.5),
        "ffn2_w_up": nrm(ks[18], (DEPTH, D_MODEL, D_FF), D_MODEL ** -0.5),
        "ffn2_w_down": nrm(ks[19], (DEPTH, D_FF, D_MODEL), D_FF ** -0.5),
        "final_norm": gain(ks[20], (D_MODEL,)),
    }


def reference(x, ffn1_norm, ffn1_w_gate, ffn1_w_up, ffn1_w_down, mix_norm, w_in, pool_w, pool_scale,
              conv_w, conv_b, dt_bias, a_log, d_skip, ssd_norm, w_out, ffn2_norm, ffn2_w_gate,
              ffn2_w_up, ffn2_w_down, final_norm):
    h = x
    for l in range(DEPTH):
        u = rms_norm(h, ffn1_norm[l])
        h = h + 0.5 * swiglu(u, ffn1_w_gate[l], ffn1_w_up[l], ffn1_w_down[l])
        u = rms_norm(h, mix_norm[l])
        h = h + hybrid_mixer(u, w_in[l], pool_w[l], pool_scale[l], conv_w[l], conv_b[l], dt_bias[l],
                             a_log[l], d_skip[l], ssd_norm[l], w_out[l])
        u = rms_norm(h, ffn2_norm[l])
        h = h + 0.5 * swiglu(u, ffn2_w_gate[l], ffn2_w_up[l], ffn2_w_down[l])
    return rms_norm(h, final_norm)
```

```python
import functools

import jax
import jax.numpy as jnp
from jax import lax
from jax.experimental import pallas as pl
from jax.experimental.pallas import tpu as pltpu

F32 = jnp.float32
BF16 = jnp.bfloat16

POOL_WINDOWS = (2, 4, 8, 16)
SSD_HEAD_DIM = 64
SSD_GROUPS = 2
SSD_STATE = 128
SSD_CONV = 4
ATTN_HEAD_DIM = 64
RMS_EPS = 1e-6

LANES = 128
SUBLANES = 8
VMEM_LIMIT = 60 * 1024 * 1024

FFN_TM = 1024
FFN_TF = 512
PROJ_TM = 256
OUT_TM = 512
POOL_TS = 512
POOL_HALO = 16
SSD_CHUNK = 256
CONV_HALO = 8
ATTN_TQ = 512
ATTN_SUB = 128
NORM_TM = 512


def _rms(x, g):
    ms = jnp.mean(x * x, axis=-1, keepdims=True)
    return x * lax.rsqrt(ms + RMS_EPS) * g


def _silu(x):
    return x * jax.nn.sigmoid(x)


def _softplus(x):
    return jnp.maximum(x, 0.0) + jnp.log1p(jnp.exp(-jnp.abs(x)))


def _split3(x):
    x1 = x.astype(BF16)
    r1 = x - x1.astype(F32)
    x2 = r1.astype(BF16)
    x3 = (r1 - x2.astype(F32)).astype(BF16)
    return x1, x2, x3


def _const_spec(shape):
    nd = len(shape)
    return pl.BlockSpec(shape, lambda *_: (0,) * nd, pipeline_mode=pl.Buffered(1))


def _params(sem):
    return pltpu.CompilerParams(dimension_semantics=sem, vmem_limit_bytes=VMEM_LIMIT)


def _ffn_kernel(h_ref, g_ref, wg_ref, wu_ref, wd_ref, o_ref, u_ref):
    @pl.when(pl.program_id(1) == 0)
    def _():
        h = h_ref[...]
        u_ref[...] = _rms(h, g_ref[...]).astype(BF16)
        o_ref[...] = h

    u = u_ref[...]
    a = jnp.dot(u, wg_ref[...], preferred_element_type=F32)
    b = jnp.dot(u, wu_ref[...], preferred_element_type=F32)
    act = (0.5 * _silu(a)) * b
    o_ref[...] += jnp.dot(act.astype(BF16), wd_ref[...], preferred_element_type=F32)


def _ffn(h, g, wg, wu, wd):
    t, d = h.shape
    f = wg.shape[1]
    tm, tf = min(FFN_TM, t), FFN_TF
    assert t % tm == 0 and f % tf == 0
    return pl.pallas_call(
        _ffn_kernel,
        grid=(t // tm, f // tf),
        in_specs=[
            pl.BlockSpec((tm, d), lambda i, j: (i, 0)),
            pl.BlockSpec((1, d), lambda i, j: (0, 0)),
            pl.BlockSpec((d, tf), lambda i, j: (0, j)),
            pl.BlockSpec((d, tf), lambda i, j: (0, j)),
            pl.BlockSpec((tf, d), lambda i, j: (j, 0)),
        ],
        out_specs=pl.BlockSpec((tm, d), lambda i, j: (i, 0)),
        out_shape=jax.ShapeDtypeStruct((t, d), F32),
        scratch_shapes=[pltpu.VMEM((tm, d), BF16)],
        compiler_params=_params(("parallel", "arbitrary")),
        name="ffn",
    )(h, g, wg, wu, wd)


def _inproj_kernel(h_ref, g_ref, wp_ref, wq_ref, p_ref, qkv_ref):
    u = _rms(h_ref[...], g_ref[...]).astype(BF16)
    p_ref[...] = jnp.dot(u, wp_ref[...], preferred_element_type=F32)
    qkv_ref[...] = jnp.dot(u, wq_ref[...], preferred_element_type=F32).astype(BF16)


def _inproj(h, g, wp, wq):
    t, d = h.shape
    npw, nq = wp.shape[1], wq.shape[1]
    tm = min(PROJ_TM, t)
    assert t % tm == 0
    return pl.pallas_call(
        _inproj_kernel,
        grid=(t // tm,),
        in_specs=[
            pl.BlockSpec((tm, d), lambda i: (i, 0)),
            _const_spec((1, d)),
            _const_spec((d, npw)),
            _const_spec((d, nq)),
        ],
        out_specs=[
            pl.BlockSpec((tm, npw), lambda i: (i, 0)),
            pl.BlockSpec((tm, nq), lambda i: (i, 0)),
        ],
        out_shape=[
            jax.ShapeDtypeStruct((t, npw), F32),
            jax.ShapeDtypeStruct((t, nq), BF16),
        ],
        compiler_params=_params(("parallel",)),
        name="inproj",
    )(h, g, wp, wq)


def _pool_kernel(x_ref, w_ref, sc_ref, o_ref, buf_ref, *, ts, group):
    s = pl.program_id(1)
    halo = POOL_HALO

    @pl.when(s == 0)
    def _():
        buf_ref[0:halo, :] = jnp.zeros((halo, buf_ref.shape[1]), F32)

    buf_ref[halo:halo + ts, :] = x_ref[...]
    pos = (s * ts + 1 + lax.broadcasted_iota(jnp.int32, (ts, 1), 0)).astype(F32)
    outs = []
    for i, w in enumerate(POOL_WINDOWS):
        cols = slice(i * group, (i + 1) * group)
        v = buf_ref[halo:halo + ts, cols]
        acc = v
        for k in range(1, w):
            acc = acc + buf_ref[halo - k:halo - k + ts, cols]
        mean = acc / jnp.minimum(pos, float(w))
        outs.append(jnp.dot((mean - v).astype(BF16), w_ref[i], preferred_element_type=F32))
    o_ref[...] = (jnp.concatenate(outs, axis=-1) * sc_ref[...]).astype(BF16)
    buf_ref[0:halo, :] = buf_ref[ts:ts + halo, :]


def _pool(p, col_blk, w_pool, scale, bsz, s_len):
    width = scale.shape[1]
    group = width // len(POOL_WINDOWS)
    ts = min(POOL_TS, s_len)
    ns = s_len // ts
    assert s_len % ts == 0 and ts >= POOL_HALO
    return pl.pallas_call(
        functools.partial(_pool_kernel, ts=ts, group=group),
        grid=(bsz, ns),
        in_specs=[
            pl.BlockSpec((ts, width), lambda b, s: (b * ns + s, col_blk)),
            _const_spec(w_pool.shape),
            _const_spec((1, width)),
        ],
        out_specs=pl.BlockSpec((ts, width), lambda b, s: (b * ns + s, 0)),
        out_shape=jax.ShapeDtypeStruct((bsz * s_len, width), BF16),
        scratch_shapes=[pltpu.VMEM((POOL_HALO + ts, width), F32)],
        compiler_params=_params(("arbitrary", "arbitrary")),
        name="pool",
    )(p, w_pool, scale)


def _ssd_kernel(z_ref, xs_ref, bc_ref, dt_ref, dtt_ref, cwx_ref, cwb_ref, cbx_ref, cbb_ref,
                dtb_ref, dtbt_ref, alog_ref, alogt_ref, dskip_ref, ng_ref,
                o_ref, xbuf_ref, bbuf_ref, state_ref, y_ref, *, chunk, heads):
    L, P, N, G = chunk, SSD_HEAD_DIM, SSD_STATE, SSD_GROUPS
    hpg = heads // G
    halo = CONV_HALO

    @pl.when(pl.program_id(1) == 0)
    def _():
        xbuf_ref[0:halo, :] = jnp.zeros((halo, xbuf_ref.shape[1]), F32)
        bbuf_ref[0:halo, :] = jnp.zeros((halo, bbuf_ref.shape[1]), F32)
        state_ref[...] = jnp.zeros(state_ref.shape, F32)

    xbuf_ref[halo:halo + L, :] = xs_ref[...]
    bbuf_ref[halo:halo + L, :] = bc_ref[...]

    def conv_silu(buf_ref, cw_ref, cb_ref):
        base = halo - (SSD_CONV - 1)
        y = buf_ref[base:base + L, :] * cw_ref[0:1, :]
        for k in range(1, SSD_CONV):
            y = y + buf_ref[base + k:base + k + L, :] * cw_ref[k:k + 1, :]
        return _silu(y + cb_ref[...])

    xs = conv_silu(xbuf_ref, cwx_ref, cbx_ref)
    bc = conv_silu(bbuf_ref, cwb_ref, cbb_ref)
    xbuf_ref[0:halo, :] = xbuf_ref[L:L + halo, :]
    bbuf_ref[0:halo, :] = bbuf_ref[L:L + halo, :]

    dt = _softplus(dt_ref[...] + dtb_ref[...])
    a = -jnp.exp(alog_ref[...]) * dt
    a_t = -jnp.exp(alogt_ref[...]) * _softplus(dtt_ref[...] + dtbt_ref[...])

    row = lax.broadcasted_iota(jnp.int32, (L, L), 0)
    col = lax.broadcasted_iota(jnp.int32, (L, L), 1)
    causal = row >= col
    tri = jnp.where(causal, 1.0, 0.0).astype(BF16)
    tri_t = jnp.where(row <= col, 1.0, 0.0).astype(BF16)
    a3, a2, a1 = reversed(_split3(a))
    a_cs = (jnp.dot(tri, a3, preferred_element_type=F32) + jnp.dot(tri, a2, preferred_element_type=F32)
            + jnp.dot(tri, a1, preferred_element_type=F32))
    t3, t2, t1 = reversed(_split3(a_t))
    at_cs = (jnp.dot(t3, tri_t, preferred_element_type=F32) + jnp.dot(t2, tri_t, preferred_element_type=F32)
             + jnp.dot(t1, tri_t, preferred_element_type=F32))

    for g in range(G):
        b_g = bc[:, g * N:(g + 1) * N].astype(BF16)
        c_g = bc[:, (G + g) * N:(G + g + 1) * N].astype(BF16)
        cb = lax.dot_general(c_g, b_g, (((1,), (1,)), ((), ())), preferred_element_type=F32)
        for e in range(hpg):
            h = g * hpg + e
            acol = a_cs[:, h:h + 1]
            arow = at_cs[h:h + 1, :]
            decay = jnp.where(causal, jnp.exp(acol - arow), 0.0)
            m = (cb * decay).astype(BF16)
            x_h = xs[:, h * P:(h + 1) * P]
            dt_h = dt[:, h:h + 1]
            xdt = x_h * dt_h
            st = state_ref[h]
            y = jnp.dot(m, xdt.astype(BF16), preferred_element_type=F32)
            y = y + jnp.exp(acol) * jnp.dot(c_g, st.astype(BF16), preferred_element_type=F32)
            a_last = a_cs[L - 1:L, h:h + 1]
            xw = (xdt * jnp.exp(a_last - acol)).astype(BF16)
            upd = lax.dot_general(b_g, xw, (((0,), (0,)), ((), ())), preferred_element_type=F32)
            state_ref[h] = jnp.exp(a_last) * st + upd
            y_ref[:, h * P:(h + 1) * P] = y

    y = y_ref[...] + dskip_ref[...] * xs
    y = y * _silu(z_ref[...])
    gw = y.shape[1] // G
    outs = []
    for g in range(G):
        yg = y[:, g * gw:(g + 1) * gw]
        outs.append(yg * lax.rsqrt(jnp.mean(yg * yg, axis=-1, keepdims=True) + RMS_EPS))
    o_ref[...] = (jnp.concatenate(outs, axis=-1) * ng_ref[...]).astype(BF16)


def _ssd(p, dt_t, blk, prm, bsz, s_len):
    heads = dt_t.shape[0]
    width = heads * SSD_HEAD_DIM
    bcw = 2 * SSD_GROUPS * SSD_STATE
    L = min(SSD_CHUNK, s_len)
    nc = s_len // L
    assert s_len % L == 0 and L >= CONV_HALO

    def tok(w, cb):
        return pl.BlockSpec((L, w), lambda b, c: (b * nc + c, cb))

    in_specs = [
        tok(width, blk["z"]), tok(width, blk["xs"]), tok(bcw, blk["bc"]), tok(LANES, blk["dt"]),
        pl.BlockSpec((heads, L), lambda b, c: (0, b * nc + c)),
        _const_spec((SSD_CONV, width)), _const_spec((SSD_CONV, bcw)),
        _const_spec((1, width)), _const_spec((1, bcw)),
        _const_spec((1, LANES)), _const_spec((heads, 1)),
        _const_spec((1, LANES)), _const_spec((heads, 1)),
        _const_spec((1, width)), _const_spec((1, width)),
    ]
    return pl.pallas_call(
        functools.partial(_ssd_kernel, chunk=L, heads=heads),
        grid=(bsz, nc),
        in_specs=in_specs,
        out_specs=pl.BlockSpec((L, width), lambda b, c: (b * nc + c, 0)),
        out_shape=jax.ShapeDtypeStruct((bsz * s_len, width), BF16),
        scratch_shapes=[
            pltpu.VMEM((CONV_HALO + L, width), F32),
            pltpu.VMEM((CONV_HALO + L, bcw), F32),
            pltpu.VMEM((heads, SSD_STATE, SSD_HEAD_DIM), F32),
            pltpu.VMEM((L, width), F32),
        ],
        compiler_params=_params(("arbitrary", "arbitrary")),
        name="ssd",
    )(p, p, p, p, dt_t, prm["cwx"], prm["cwb"], prm["cbx"], prm["cbb"], prm["dtb"], prm["dtbt"],
      prm["alog"], prm["alogt"], prm["dskip"], prm["ng"])


def _attn_kernel(q_ref, k_ref, v_ref, o_ref, kbd_ref, vbd_ref, acc_ref, r_ref, *, tq, nsub):
    sub, hd = ATTN_SUB, ATTN_HEAD_DIM
    i = pl.program_id(2)
    nblk = k_ref.shape[0] // sub

    @pl.when(i == 0)
    def _():
        first = lax.broadcasted_iota(jnp.int32, (sub, 2 * hd), 1) < hd
        zero = jnp.zeros((sub, 2 * hd), BF16)

        def build(n, carry):
            kb = k_ref[pl.ds(n * sub, sub), :]
            vb = v_ref[pl.ds(n * sub, sub), :]
            kbd_ref[n] = jnp.concatenate([jnp.where(first, kb, zero), jnp.where(first, zero, kb)], axis=0)
            vbd_ref[n] = jnp.concatenate([jnp.where(first, vb, zero), jnp.where(first, zero, vb)], axis=0)
            return carry

        lax.fori_loop(0, nblk, build, 0)

    q = q_ref[...]
    rj = lax.broadcasted_iota(jnp.int32, (2 * sub, 2 * sub), 0) & (sub - 1)
    cj = lax.broadcasted_iota(jnp.int32, (2 * sub, 2 * sub), 1)
    uu = jnp.where((cj >= sub) | (rj >= cj), 1.0, 0.0).astype(BF16)
    acc_ref[...] = jnp.zeros(acc_ref.shape, F32)
    r_ref[...] = jnp.zeros(r_ref.shape, F32)

    def chunk(cidx, masked):
        r = r_ref[...]
        acc = acc_ref[...]
        if masked:
            qpos = lax.broadcasted_iota(jnp.int32, (tq, 2 * sub), 0)
            kpos = lax.broadcasted_iota(jnp.int32, (tq, 2 * sub), 1) & (sub - 1)
        for sb in reversed(range(nsub)):
            n = cidx * nsub + sb
            z = lax.dot_general(q, kbd_ref[n], (((1,), (1,)), ((), ())), preferred_element_type=F32)
            lm = -(jnp.maximum(z, 0.0) + jnp.log(1.0 + jnp.exp(-jnp.abs(z))))
            if masked:
                valid = (kpos + sb * sub) < qpos
                lm = jnp.where(valid, lm, 0.0)
            hi = lm.astype(BF16)
            lo = (lm - hi.astype(F32)).astype(BF16)
            c0 = jnp.dot(jnp.concatenate([hi[:, :sub], lo[:, :sub]], axis=1), uu, preferred_element_type=F32)
            c1 = jnp.dot(jnp.concatenate([hi[:, sub:], lo[:, sub:]], axis=1), uu, preferred_element_type=F32)
            ci = jnp.concatenate([c0[:, :sub], c1[:, :sub]], axis=1)
            w = jnp.exp(z + ci + r)
            if masked:
                w = jnp.where(valid, w, 0.0)
            acc = acc + jnp.dot(w.astype(BF16), vbd_ref[n], preferred_element_type=F32)
            r = r + jnp.concatenate([c0[:, sub:], c1[:, sub:]], axis=1)
        r_ref[...] = r
        acc_ref[...] = acc

    chunk(i, True)

    def body(t, carry):
        chunk(i - 1 - t, False)
        return carry

    lax.fori_loop(0, i, body, 0)
    o_ref[...] = acc_ref[...].astype(BF16)


def _attn(qkv, bsz, s_len):
    width = qkv.shape[1] // 3
    pair = 2 * ATTN_HEAD_DIM
    npair = width // pair
    tq = min(ATTN_TQ, s_len)
    nq = s_len // tq
    nsub = tq // ATTN_SUB
    assert s_len % tq == 0 and tq % ATTN_SUB == 0 and width % pair == 0 and pair == LANES
    nblk = s_len // ATTN_SUB
    return pl.pallas_call(
        functools.partial(_attn_kernel, tq=tq, nsub=nsub),
        grid=(bsz, npair, nq),
        in_specs=[
            pl.BlockSpec((tq, pair), lambda b, p, i: (b * nq + i, p)),
            pl.BlockSpec((s_len, pair), lambda b, p, i: (b, npair + p)),
            pl.BlockSpec((s_len, pair), lambda b, p, i: (b, 2 * npair + p)),
        ],
        out_specs=pl.BlockSpec((tq, pair), lambda b, p, i: (b * nq + i, p)),
        out_shape=jax.ShapeDtypeStruct((bsz * s_len, width), BF16),
        scratch_shapes=[
            pltpu.VMEM((nblk, 2 * ATTN_SUB, pair), BF16),
            pltpu.VMEM((nblk, 2 * ATTN_SUB, pair), BF16),
            pltpu.VMEM((tq, pair), F32),
            pltpu.VMEM((tq, 2 * ATTN_SUB), F32),
        ],
        compiler_params=_params(("arbitrary", "arbitrary", "arbitrary")),
        name="attn",
    )(qkv, qkv, qkv)


def _outproj_kernel(h_ref, p_ref, s_ref, a_ref, wp_ref, ws_ref, wa_ref, o_ref):
    acc = jnp.dot(p_ref[...], wp_ref[...], preferred_element_type=F32)
    acc = acc + jnp.dot(s_ref[...], ws_ref[...], preferred_element_type=F32)
    acc = acc + jnp.dot(a_ref[...], wa_ref[...], preferred_element_type=F32)
    o_ref[...] = h_ref[...] + acc


def _outproj(h, pool_o, ssd_o, attn_o, wp, ws, wa):
    t, d = h.shape
    tm = min(OUT_TM, t)
    assert t % tm == 0

    def tok(w):
        return pl.BlockSpec((tm, w), lambda i: (i, 0))

    return pl.pallas_call(
        _outproj_kernel,
        grid=(t // tm,),
        in_specs=[tok(d), tok(pool_o.shape[1]), tok(ssd_o.shape[1]), tok(attn_o.shape[1]),
                  _const_spec(wp.shape), _const_spec(ws.shape), _const_spec(wa.shape)],
        out_specs=tok(d),
        out_shape=jax.ShapeDtypeStruct((t, d), F32),
        compiler_params=_params(("parallel",)),
        name="outproj",
    )(h, pool_o, ssd_o, attn_o, wp, ws, wa)


def _norm_kernel(h_ref, g_ref, o_ref):
    o_ref[...] = _rms(h_ref[...], g_ref[...])


def _final_norm(h, g):
    t, d = h.shape
    tm = min(NORM_TM, t)
    assert t % tm == 0
    return pl.pallas_call(
        _norm_kernel,
        grid=(t // tm,),
        in_specs=[pl.BlockSpec((tm, d), lambda i: (i, 0)), _const_spec((1, d))],
        out_specs=pl.BlockSpec((tm, d), lambda i: (i, 0)),
        out_shape=jax.ShapeDtypeStruct((t, d), F32),
        compiler_params=_params(("parallel",)),
        name="final_norm",
    )(h, g)


def _pad_lanes(v, fill=0.0):
    return jnp.pad(v, (0, LANES - v.shape[0]), constant_values=fill)[None, :]


def kernel(x, ffn1_norm, ffn1_w_gate, ffn1_w_up, ffn1_w_down, mix_norm, w_in, pool_w, pool_scale, conv_w, conv_b, dt_bias, a_log, d_skip, ssd_norm, w_out, ffn2_norm, ffn2_w_gate, ffn2_w_up, ffn2_w_down, final_norm):
    bsz, s_len, d = x.shape
    depth = w_in.shape[0]
    pool_width = pool_scale.shape[1]
    ssd_width = ssd_norm.shape[1]
    heads = dt_bias.shape[1]
    xbc_width = conv_w.shape[2]
    bcw = xbc_width - ssd_width
    attn_width = (w_in.shape[2] - pool_width - ssd_width - xbc_width - heads) // 3
    assert heads <= LANES and heads * SSD_HEAD_DIM == ssd_width and bcw == 2 * SSD_GROUPS * SSD_STATE
    o_pool = 0
    o_z = o_pool + pool_width
    o_xbc = o_z + ssd_width
    o_dt = o_xbc + xbc_width
    o_q = o_dt + heads
    slab_off = {"z": 0, "xs": ssd_width, "bc": 2 * ssd_width, "pool": 2 * ssd_width + bcw,
                "dt": 2 * ssd_width + bcw + pool_width}
    slab_w = {"z": ssd_width, "xs": ssd_width, "bc": bcw, "pool": pool_width, "dt": LANES}
    blk = {k: slab_off[k] // slab_w[k] for k in slab_off}
    assert all(slab_off[k] % slab_w[k] == 0 for k in slab_off)
    q_scale = ATTN_HEAD_DIM ** -0.5

    h = x.reshape(bsz * s_len, d)
    for l in range(depth):
        h = _ffn(h, ffn1_norm[l][None, :], ffn1_w_gate[l].astype(BF16), ffn1_w_up[l].astype(BF16),
                 ffn1_w_down[l].astype(BF16))

        wl = w_in[l]
        wp = jnp.concatenate([
            wl[:, o_z:o_z + ssd_width], wl[:, o_xbc:o_xbc + xbc_width], wl[:, o_pool:o_pool + pool_width],
            jnp.pad(wl[:, o_dt:o_dt + heads], ((0, 0), (0, LANES - heads)))], axis=1).astype(BF16)
        wq = jnp.concatenate([wl[:, o_q:o_q + attn_width] * q_scale, wl[:, o_q + attn_width:]], axis=1).astype(BF16)
        p, qkv = _inproj(h, mix_norm[l][None, :], wp, wq)

        pool_o = _pool(p, blk["pool"], pool_w[l].astype(BF16), pool_scale[l][None, :], bsz, s_len)

        dt_t = p[:, slab_off["dt"]:slab_off["dt"] + heads].T
        prm = {
            "cwx": conv_w[l][:, :ssd_width], "cwb": conv_w[l][:, ssd_width:],
            "cbx": conv_b[l][None, :ssd_width], "cbb": conv_b[l][None, ssd_width:],
            "dtb": _pad_lanes(dt_bias[l]), "dtbt": dt_bias[l][:, None],
            "alog": _pad_lanes(a_log[l]), "alogt": a_log[l][:, None],
            "dskip": jnp.repeat(d_skip[l], SSD_HEAD_DIM)[None, :], "ng": ssd_norm[l][None, :],
        }
        ssd_o = _ssd(p, dt_t, blk, prm, bsz, s_len)

        attn_o = _attn(qkv, bsz, s_len)

        wo = w_out[l].astype(BF16)
        h = _outproj(h, pool_o, ssd_o, attn_o, wo[:pool_width], wo[pool_width:pool_width + ssd_width],
                     wo[pool_width + ssd_width:])

        h = _ffn(h, ffn2_norm[l][None, :], ffn2_w_gate[l].astype(BF16), ffn2_w_up[l].astype(BF16),
                 ffn2_w_down[l].astype(BF16))
    return _final_norm(h, final_norm[None, :]).reshape(bsz, s_len, d)
```

```python
import functools

import jax
import jax.numpy as jnp
from jax import lax
from jax.experimental import pallas as pl
from jax.experimental.pallas import tpu as pltpu

F32 = jnp.float32
BF16 = jnp.bfloat16

POOL_WINDOWS = (2, 4, 8, 16)
SSD_HEAD_DIM = 64
SSD_GROUPS = 2
SSD_STATE = 128
SSD_CONV = 4
ATTN_HEAD_DIM = 64
RMS_EPS = 1e-6
LOG2E = 1.4426950408889634
EXP2_CLAMP = 126.0
SKIP_BITS = 160.0

LANES = 128
SUBLANES = 8
VMEM_LIMIT = 60 * 1024 * 1024

FFN_TM = 1024
FFN_TF = 512
PROJ_TM = 256
OUT_TM = 512
POOL_TS = 512
POOL_HALO = 16
SSD_CHUNK = 256
CONV_HALO = 8
ATTN_TQ = 512
ATTN_SUB = 128
NORM_TM = 512


def _rms(x, g):
    ms = jnp.mean(x * x, axis=-1, keepdims=True)
    return x * lax.rsqrt(ms + RMS_EPS) * g


def _silu(x):
    return x * jax.nn.sigmoid(x)


def _softplus(x):
    return jnp.maximum(x, 0.0) + jnp.log1p(jnp.exp(-jnp.abs(x)))


def _split3(x):
    x1 = x.astype(BF16)
    r1 = x - x1.astype(F32)
    x2 = r1.astype(BF16)
    x3 = (r1 - x2.astype(F32)).astype(BF16)
    return x1, x2, x3


def _const_spec(shape):
    nd = len(shape)
    return pl.BlockSpec(shape, lambda *_: (0,) * nd, pipeline_mode=pl.Buffered(1))


def _params(sem):
    return pltpu.CompilerParams(dimension_semantics=sem, vmem_limit_bytes=VMEM_LIMIT)


def _ffn_kernel(h_ref, g_ref, wg_ref, wu_ref, wd_ref, o_ref, u_ref):
    @pl.when(pl.program_id(1) == 0)
    def _():
        h = h_ref[...]
        u_ref[...] = _rms(h, g_ref[...]).astype(BF16)
        o_ref[...] = h

    u = u_ref[...]
    a = jnp.dot(u, wg_ref[...], preferred_element_type=F32)
    b = jnp.dot(u, wu_ref[...], preferred_element_type=F32)
    act = (0.5 * _silu(a)) * b
    o_ref[...] += jnp.dot(act.astype(BF16), wd_ref[...], preferred_element_type=F32)


def _ffn(h, g, wg, wu, wd):
    t, d = h.shape
    f = wg.shape[1]
    tm, tf = min(FFN_TM, t), FFN_TF
    assert t % tm == 0 and f % tf == 0
    return pl.pallas_call(
        _ffn_kernel,
        grid=(t // tm, f // tf),
        in_specs=[
            pl.BlockSpec((tm, d), lambda i, j: (i, 0)),
            pl.BlockSpec((1, d), lambda i, j: (0, 0)),
            pl.BlockSpec((d, tf), lambda i, j: (0, j)),
            pl.BlockSpec((d, tf), lambda i, j: (0, j)),
            pl.BlockSpec((tf, d), lambda i, j: (j, 0)),
        ],
        out_specs=pl.BlockSpec((tm, d), lambda i, j: (i, 0)),
        out_shape=jax.ShapeDtypeStruct((t, d), F32),
        scratch_shapes=[pltpu.VMEM((tm, d), BF16)],
        compiler_params=_params(("parallel", "arbitrary")),
        name="ffn",
    )(h, g, wg, wu, wd)


def _inproj_kernel(h_ref, g_ref, wp_ref, wq_ref, p_ref, qkv_ref):
    u = _rms(h_ref[...], g_ref[...]).astype(BF16)
    p_ref[...] = jnp.dot(u, wp_ref[...], preferred_element_type=F32)
    qkv_ref[...] = jnp.dot(u, wq_ref[...], preferred_element_type=F32).astype(BF16)


def _inproj(h, g, wp, wq):
    t, d = h.shape
    npw, nq = wp.shape[1], wq.shape[1]
    tm = min(PROJ_TM, t)
    assert t % tm == 0
    return pl.pallas_call(
        _inproj_kernel,
        grid=(t // tm,),
        in_specs=[
            pl.BlockSpec((tm, d), lambda i: (i, 0)),
            _const_spec((1, d)),
            _const_spec((d, npw)),
            _const_spec((d, nq)),
        ],
        out_specs=[
            pl.BlockSpec((tm, npw), lambda i: (i, 0)),
            pl.BlockSpec((tm, nq), lambda i: (i, 0)),
        ],
        out_shape=[
            jax.ShapeDtypeStruct((t, npw), F32),
            jax.ShapeDtypeStruct((t, nq), BF16),
        ],
        compiler_params=_params(("parallel",)),
        name="inproj",
    )(h, g, wp, wq)


def _pool_kernel(x_ref, w_ref, sc_ref, o_ref, buf_ref, *, ts, group):
    s = pl.program_id(1)
    halo = POOL_HALO

    @pl.when(s == 0)
    def _():
        buf_ref[0:halo, :] = jnp.zeros((halo, buf_ref.shape[1]), F32)

    buf_ref[halo:halo + ts, :] = x_ref[...]
    pos = (s * ts + 1 + lax.broadcasted_iota(jnp.int32, (ts, 1), 0)).astype(F32)
    outs = []
    for i, w in enumerate(POOL_WINDOWS):
        cols = slice(i * group, (i + 1) * group)
        v = buf_ref[halo:halo + ts, cols]
        acc = v
        for k in range(1, w):
            acc = acc + buf_ref[halo - k:halo - k + ts, cols]
        mean = acc / jnp.minimum(pos, float(w))
        outs.append(jnp.dot((mean - v).astype(BF16), w_ref[i], preferred_element_type=F32))
    o_ref[...] = (jnp.concatenate(outs, axis=-1) * sc_ref[...]).astype(BF16)
    buf_ref[0:halo, :] = buf_ref[ts:ts + halo, :]


def _pool(p, col_blk, w_pool, scale, bsz, s_len):
    width = scale.shape[1]
    group = width // len(POOL_WINDOWS)
    ts = min(POOL_TS, s_len)
    ns = s_len // ts
    assert s_len % ts == 0 and ts >= POOL_HALO
    return pl.pallas_call(
        functools.partial(_pool_kernel, ts=ts, group=group),
        grid=(bsz, ns),
        in_specs=[
            pl.BlockSpec((ts, width), lambda b, s: (b * ns + s, col_blk)),
            _const_spec(w_pool.shape),
            _const_spec((1, width)),
        ],
        out_specs=pl.BlockSpec((ts, width), lambda b, s: (b * ns + s, 0)),
        out_shape=jax.ShapeDtypeStruct((bsz * s_len, width), BF16),
        scratch_shapes=[pltpu.VMEM((POOL_HALO + ts, width), F32)],
        compiler_params=_params(("arbitrary", "arbitrary")),
        name="pool",
    )(p, w_pool, scale)


def _ssd_kernel(z_ref, xs_ref, bc_ref, dt_ref, dtt_ref, cwx_ref, cwb_ref, cbx_ref, cbb_ref,
                dtb_ref, dtbt_ref, alog_ref, alogt_ref, dskip_ref, ng_ref,
                o_ref, xbuf_ref, bbuf_ref, state_ref, y_ref, *, chunk, heads):
    L, P, N, G = chunk, SSD_HEAD_DIM, SSD_STATE, SSD_GROUPS
    hpg = heads // G
    halo = CONV_HALO

    @pl.when(pl.program_id(1) == 0)
    def _():
        xbuf_ref[0:halo, :] = jnp.zeros((halo, xbuf_ref.shape[1]), F32)
        bbuf_ref[0:halo, :] = jnp.zeros((halo, bbuf_ref.shape[1]), F32)
        state_ref[...] = jnp.zeros(state_ref.shape, F32)

    xbuf_ref[halo:halo + L, :] = xs_ref[...]
    bbuf_ref[halo:halo + L, :] = bc_ref[...]

    def conv_silu(buf_ref, cw_ref, cb_ref):
        base = halo - (SSD_CONV - 1)
        y = buf_ref[base:base + L, :] * cw_ref[0:1, :]
        for k in range(1, SSD_CONV):
            y = y + buf_ref[base + k:base + k + L, :] * cw_ref[k:k + 1, :]
        return _silu(y + cb_ref[...])

    xs = conv_silu(xbuf_ref, cwx_ref, cbx_ref)
    bc = conv_silu(bbuf_ref, cwb_ref, cbb_ref)
    xbuf_ref[0:halo, :] = xbuf_ref[L:L + halo, :]
    bbuf_ref[0:halo, :] = bbuf_ref[L:L + halo, :]

    dt = _softplus(dt_ref[...] + dtb_ref[...])
    a = -jnp.exp(alog_ref[...]) * dt
    a_t = -jnp.exp(alogt_ref[...]) * _softplus(dtt_ref[...] + dtbt_ref[...])

    row = lax.broadcasted_iota(jnp.int32, (L, L), 0)
    col = lax.broadcasted_iota(jnp.int32, (L, L), 1)
    causal = row >= col
    tri = jnp.where(causal, 1.0, 0.0).astype(BF16)
    tri_t = jnp.where(row <= col, 1.0, 0.0).astype(BF16)
    a3, a2, a1 = reversed(_split3(a))
    a_cs = (jnp.dot(tri, a3, preferred_element_type=F32) + jnp.dot(tri, a2, preferred_element_type=F32)
            + jnp.dot(tri, a1, preferred_element_type=F32))
    t3, t2, t1 = reversed(_split3(a_t))
    at_cs = (jnp.dot(t3, tri_t, preferred_element_type=F32) + jnp.dot(t2, tri_t, preferred_element_type=F32)
             + jnp.dot(t1, tri_t, preferred_element_type=F32))

    for g in range(G):
        b_g = bc[:, g * N:(g + 1) * N].astype(BF16)
        c_g = bc[:, (G + g) * N:(G + g + 1) * N].astype(BF16)
        cb = lax.dot_general(c_g, b_g, (((1,), (1,)), ((), ())), preferred_element_type=F32)
        for e in range(hpg):
            h = g * hpg + e
            acol = a_cs[:, h:h + 1]
            arow = at_cs[h:h + 1, :]
            decay = jnp.where(causal, jnp.exp(acol - arow), 0.0)
            m = (cb * decay).astype(BF16)
            x_h = xs[:, h * P:(h + 1) * P]
            dt_h = dt[:, h:h + 1]
            xdt = x_h * dt_h
            st = state_ref[h]
            y = jnp.dot(m, xdt.astype(BF16), preferred_element_type=F32)
            y = y + jnp.exp(acol) * jnp.dot(c_g, st.astype(BF16), preferred_element_type=F32)
            a_last = a_cs[L - 1:L, h:h + 1]
            xw = (xdt * jnp.exp(a_last - acol)).astype(BF16)
            upd = lax.dot_general(b_g, xw, (((0,), (0,)), ((), ())), preferred_element_type=F32)
            state_ref[h] = jnp.exp(a_last) * st + upd
            y_ref[:, h * P:(h + 1) * P] = y

    y = y_ref[...] + dskip_ref[...] * xs
    y = y * _silu(z_ref[...])
    gw = y.shape[1] // G
    outs = []
    for g in range(G):
        yg = y[:, g * gw:(g + 1) * gw]
        outs.append(yg * lax.rsqrt(jnp.mean(yg * yg, axis=-1, keepdims=True) + RMS_EPS))
    o_ref[...] = (jnp.concatenate(outs, axis=-1) * ng_ref[...]).astype(BF16)


def _ssd(p, dt_t, blk, prm, bsz, s_len):
    heads = dt_t.shape[0]
    width = heads * SSD_HEAD_DIM
    bcw = 2 * SSD_GROUPS * SSD_STATE
    L = min(SSD_CHUNK, s_len)
    nc = s_len // L
    assert s_len % L == 0 and L >= CONV_HALO

    def tok(w, cb):
        return pl.BlockSpec((L, w), lambda b, c: (b * nc + c, cb))

    in_specs = [
        tok(width, blk["z"]), tok(width, blk["xs"]), tok(bcw, blk["bc"]), tok(LANES, blk["dt"]),
        pl.BlockSpec((heads, L), lambda b, c: (0, b * nc + c)),
        _const_spec((SSD_CONV, width)), _const_spec((SSD_CONV, bcw)),
        _const_spec((1, width)), _const_spec((1, bcw)),
        _const_spec((1, LANES)), _const_spec((heads, 1)),
        _const_spec((1, LANES)), _const_spec((heads, 1)),
        _const_spec((1, width)), _const_spec((1, width)),
    ]
    return pl.pallas_call(
        functools.partial(_ssd_kernel, chunk=L, heads=heads),
        grid=(bsz, nc),
        in_specs=in_specs,
        out_specs=pl.BlockSpec((L, width), lambda b, c: (b * nc + c, 0)),
        out_shape=jax.ShapeDtypeStruct((bsz * s_len, width), BF16),
        scratch_shapes=[
            pltpu.VMEM((CONV_HALO + L, width), F32),
            pltpu.VMEM((CONV_HALO + L, bcw), F32),
            pltpu.VMEM((heads, SSD_STATE, SSD_HEAD_DIM), F32),
            pltpu.VMEM((L, width), F32),
        ],
        compiler_params=_params(("arbitrary", "arbitrary")),
        name="ssd",
    )(p, p, p, p, dt_t, prm["cwx"], prm["cwb"], prm["cbx"], prm["cbb"], prm["dtb"], prm["dtbt"],
      prm["alog"], prm["alogt"], prm["dskip"], prm["ng"])


def _attn_kernel(q_ref, k_ref, v_ref, o_ref, kbd_ref, vbd_ref, acc_ref, rn_ref, *, tq, nsub):
    sub, hd = ATTN_SUB, ATTN_HEAD_DIM
    i = pl.program_id(2)
    nblk = k_ref.shape[0] // sub

    @pl.when(i == 0)
    def _():
        first = lax.broadcasted_iota(jnp.int32, (sub, 2 * hd), 1) < hd
        zero = jnp.zeros((sub, 2 * hd), BF16)

        def build(n, carry):
            kb = k_ref[pl.ds(n * sub, sub), :]
            vb = v_ref[pl.ds(n * sub, sub), :]
            kbd_ref[n] = jnp.concatenate([jnp.where(first, kb, zero), jnp.where(first, zero, kb)], axis=0)
            vbd_ref[n] = jnp.concatenate([jnp.where(first, vb, zero), jnp.where(first, zero, vb)], axis=0)
            return carry

        lax.fori_loop(0, nblk, build, 0)

    q = q_ref[...]
    rj = lax.broadcasted_iota(jnp.int32, (2 * sub, 2 * sub), 0) & (sub - 1)
    cj = lax.broadcasted_iota(jnp.int32, (2 * sub, 2 * sub), 1)
    tri = (cj < sub) & (rj >= cj)
    uu0 = jnp.where(tri | ((cj >= sub) & (cj < sub + hd)), 1.0, 0.0).astype(BF16)
    uu1 = jnp.where(tri | (cj >= sub + hd), 1.0, 0.0).astype(BF16)
    acc_ref[...] = jnp.zeros(acc_ref.shape, F32)
    rn_ref[...] = jnp.zeros(rn_ref.shape, F32)

    diag_valid = ((lax.broadcasted_iota(jnp.int32, (sub, 2 * sub), 1) & (sub - 1))
                  < lax.broadcasted_iota(jnp.int32, (sub, 2 * sub), 0))

    def mask_top(x, on_diag):
        if not on_diag:
            return x
        top = jnp.where(diag_valid, x[:sub], 0.0)
        return top if x.shape[0] == sub else jnp.concatenate([top, x[sub:]], axis=0)

    def sub_block(n, r0, on_diag):
        z = lax.dot_general(q[r0:], kbd_ref[n], (((1,), (1,)), ((), ())), preferred_element_type=F32)
        sp = jnp.maximum(z, jnp.log(1.0 + jnp.exp2(jnp.minimum(z, EXP2_CLAMP))) * LOG2E)
        sp = mask_top(sp, on_diag)
        hi = sp.astype(BF16)
        lo = (sp - hi.astype(F32)).astype(BF16)
        c0 = jnp.dot(jnp.concatenate([hi[:, :sub], lo[:, :sub]], axis=1), uu0, preferred_element_type=F32)
        c1 = jnp.dot(jnp.concatenate([hi[:, sub:], lo[:, sub:]], axis=1), uu1, preferred_element_type=F32)
        w = mask_top(jnp.exp2(z - jnp.concatenate([c0[:, :sub], c1[:, :sub]], axis=1)), on_diag)
        pv = jnp.dot(w.astype(BF16), vbd_ref[n], preferred_element_type=F32)
        rn = rn_ref[r0:, :]
        acc_ref[r0:, :] = acc_ref[r0:, :] + pv * jnp.exp2(rn)
        rn_ref[r0:, :] = rn - (c0[:, sub:] + c1[:, sub:])

    def chunk(cidx, diagonal):
        for sb in reversed(range(nsub)):
            sub_block(cidx * nsub + sb, sb * sub if diagonal else 0, diagonal)

    chunk(i, True)

    def more(carry):
        t, live = carry
        return (t < i) & (live > 0)

    def body(carry):
        t, _ = carry
        chunk(i - 1 - t, False)
        return t + 1, (jnp.max(rn_ref[...]) > -SKIP_BITS).astype(jnp.int32)

    lax.while_loop(more, body, (jnp.int32(0), jnp.int32(1)))
    o_ref[...] = acc_ref[...].astype(BF16)


def _attn(qkv, bsz, s_len):
    width = qkv.shape[1] // 3
    pair = 2 * ATTN_HEAD_DIM
    npair = width // pair
    tq = min(ATTN_TQ, s_len)
    nq = s_len // tq
    nsub = tq // ATTN_SUB
    assert s_len % tq == 0 and tq % ATTN_SUB == 0 and width % pair == 0 and pair == LANES
    nblk = s_len // ATTN_SUB
    return pl.pallas_call(
        functools.partial(_attn_kernel, tq=tq, nsub=nsub),
        grid=(bsz, npair, nq),
        in_specs=[
            pl.BlockSpec((tq, pair), lambda b, p, i: (b * nq + i, p)),
            pl.BlockSpec((s_len, pair), lambda b, p, i: (b, npair + p)),
            pl.BlockSpec((s_len, pair), lambda b, p, i: (b, 2 * npair + p)),
        ],
        out_specs=pl.BlockSpec((tq, pair), lambda b, p, i: (b * nq + i, p)),
        out_shape=jax.ShapeDtypeStruct((bsz * s_len, width), BF16),
        scratch_shapes=[
            pltpu.VMEM((nblk, 2 * ATTN_SUB, pair), BF16),
            pltpu.VMEM((nblk, 2 * ATTN_SUB, pair), BF16),
            pltpu.VMEM((tq, pair), F32),
            pltpu.VMEM((tq, pair), F32),
        ],
        compiler_params=_params(("arbitrary", "arbitrary", "arbitrary")),
        name="attn",
    )(qkv, qkv, qkv)


def _outproj_kernel(h_ref, p_ref, s_ref, a_ref, wp_ref, ws_ref, wa_ref, o_ref):
    acc = jnp.dot(p_ref[...], wp_ref[...], preferred_element_type=F32)
    acc = acc + jnp.dot(s_ref[...], ws_ref[...], preferred_element_type=F32)
    acc = acc + jnp.dot(a_ref[...], wa_ref[...], preferred_element_type=F32)
    o_ref[...] = h_ref[...] + acc


def _outproj(h, pool_o, ssd_o, attn_o, wp, ws, wa):
    t, d = h.shape
    tm = min(OUT_TM, t)
    assert t % tm == 0

    def tok(w):
        return pl.BlockSpec((tm, w), lambda i: (i, 0))

    return pl.pallas_call(
        _outproj_kernel,
        grid=(t // tm,),
        in_specs=[tok(d), tok(pool_o.shape[1]), tok(ssd_o.shape[1]), tok(attn_o.shape[1]),
                  _const_spec(wp.shape), _const_spec(ws.shape), _const_spec(wa.shape)],
        out_specs=tok(d),
        out_shape=jax.ShapeDtypeStruct((t, d), F32),
        compiler_params=_params(("parallel",)),
        name="outproj",
    )(h, pool_o, ssd_o, attn_o, wp, ws, wa)


def _norm_kernel(h_ref, g_ref, o_ref):
    o_ref[...] = _rms(h_ref[...], g_ref[...])


def _final_norm(h, g):
    t, d = h.shape
    tm = min(NORM_TM, t)
    assert t % tm == 0
    return pl.pallas_call(
        _norm_kernel,
        grid=(t // tm,),
        in_specs=[pl.BlockSpec((tm, d), lambda i: (i, 0)), _const_spec((1, d))],
        out_specs=pl.BlockSpec((tm, d), lambda i: (i, 0)),
        out_shape=jax.ShapeDtypeStruct((t, d), F32),
        compiler_params=_params(("parallel",)),
        name="final_norm",
    )(h, g)


def _pad_lanes(v, fill=0.0):
    return jnp.pad(v, (0, LANES - v.shape[0]), constant_values=fill)[None, :]


def kernel(x, ffn1_norm, ffn1_w_gate, ffn1_w_up, ffn1_w_down, mix_norm, w_in, pool_w, pool_scale, conv_w, conv_b, dt_bias, a_log, d_skip, ssd_norm, w_out, ffn2_norm, ffn2_w_gate, ffn2_w_up, ffn2_w_down, final_norm):
    bsz, s_len, d = x.shape
    depth = w_in.shape[0]
    pool_width = pool_scale.shape[1]
    ssd_width = ssd_norm.shape[1]
    heads = dt_bias.shape[1]
    xbc_width = conv_w.shape[2]
    bcw = xbc_width - ssd_width
    attn_width = (w_in.shape[2] - pool_width - ssd_width - xbc_width - heads) // 3
    assert heads <= LANES and heads * SSD_HEAD_DIM == ssd_width and bcw == 2 * SSD_GROUPS * SSD_STATE
    o_pool = 0
    o_z = o_pool + pool_width
    o_xbc = o_z + ssd_width
    o_dt = o_xbc + xbc_width
    o_q = o_dt + heads
    slab_off = {"z": 0, "xs": ssd_width, "bc": 2 * ssd_width, "pool": 2 * ssd_width + bcw,
                "dt": 2 * ssd_width + bcw + pool_width}
    slab_w = {"z": ssd_width, "xs": ssd_width, "bc": bcw, "pool": pool_width, "dt": LANES}
    blk = {k: slab_off[k] // slab_w[k] for k in slab_off}
    assert all(slab_off[k] % slab_w[k] == 0 for k in slab_off)
    q_scale = ATTN_HEAD_DIM ** -0.5 * LOG2E

    h = x.reshape(bsz * s_len, d)
    for l in range(depth):
        h = _ffn(h, ffn1_norm[l][None, :], ffn1_w_gate[l].astype(BF16), ffn1_w_up[l].astype(BF16),
                 ffn1_w_down[l].astype(BF16))

        wl = w_in[l]
        wp = jnp.concatenate([
            wl[:, o_z:o_z + ssd_width], wl[:, o_xbc:o_xbc + xbc_width], wl[:, o_pool:o_pool + pool_width],
            jnp.pad(wl[:, o_dt:o_dt + heads], ((0, 0), (0, LANES - heads)))], axis=1).astype(BF16)
        wq = jnp.concatenate([wl[:, o_q:o_q + attn_width] * q_scale, wl[:, o_q + attn_width:]], axis=1).astype(BF16)
        p, qkv = _inproj(h, mix_norm[l][None, :], wp, wq)

        pool_o = _pool(p, blk["pool"], pool_w[l].astype(BF16), pool_scale[l][None, :], bsz, s_len)

        dt_t = p[:, slab_off["dt"]:slab_off["dt"] + heads].T
        prm = {
            "cwx": conv_w[l][:, :ssd_width], "cwb": conv_w[l][:, ssd_width:],
            "cbx": conv_b[l][None, :ssd_width], "cbb": conv_b[l][None, ssd_width:],
            "dtb": _pad_lanes(dt_bias[l]), "dtbt": dt_bias[l][:, None],
            "alog": _pad_lanes(a_log[l]), "alogt": a_log[l][:, None],
            "dskip": jnp.repeat(d_skip[l], SSD_HEAD_DIM)[None, :], "ng": ssd_norm[l][None, :],
        }
        ssd_o = _ssd(p, dt_t, blk, prm, bsz, s_len)

        attn_o = _attn(qkv, bsz, s_len)

        wo = w_out[l].astype(BF16)
        h = _outproj(h, pool_o, ssd_o, attn_o, wo[:pool_width], wo[pool_width:pool_width + ssd_width],
                     wo[pool_width + ssd_width:])

        h = _ffn(h, ffn2_norm[l][None, :], ffn2_w_gate[l].astype(BF16), ffn2_w_up[l].astype(BF16),
                 ffn2_w_down[l].astype(BF16))
    return _final_norm(h, final_norm[None, :]).reshape(bsz, s_len, d)
```

```python
import functools

import jax
import jax.numpy as jnp
from jax import lax
from jax.experimental import pallas as pl
from jax.experimental.pallas import tpu as pltpu

F32 = jnp.float32
BF16 = jnp.bfloat16

POOL_WINDOWS = (2, 4, 8, 16)
SSD_HEAD_DIM = 64
SSD_GROUPS = 2
SSD_STATE = 128
SSD_CONV = 4
ATTN_HEAD_DIM = 64
RMS_EPS = 1e-6
LOG2E = 1.4426950408889634
EXP2_CLAMP = 126.0
SKIP_BITS = 160.0

LANES = 128
SUBLANES = 8
VMEM_LIMIT = 60 * 1024 * 1024

FFN_TM = 1024
FFN_TF = 512
PROJ_TM = 256
OUT_TM = 512
POOL_TS = 512
POOL_HALO = 16
SSD_CHUNK = 256
CONV_HALO = 8
ATTN_TQ = 512
ATTN_SUB = 128


def _rms(x, g):
    ms = jnp.mean(x * x, axis=-1, keepdims=True)
    return x * lax.rsqrt(ms + RMS_EPS) * g


def _silu(x):
    h = 0.5 * x
    return h + h * jnp.tanh(h)


def _softplus(x):
    return jnp.maximum(x, 0.0) + jnp.log1p(jnp.exp(-jnp.abs(x)))


def _split3(x):
    x1 = x.astype(BF16)
    r1 = x - x1.astype(F32)
    x2 = r1.astype(BF16)
    x3 = (r1 - x2.astype(F32)).astype(BF16)
    return x1, x2, x3


def _const_spec(shape):
    nd = len(shape)
    return pl.BlockSpec(shape, lambda *_: (0,) * nd, pipeline_mode=pl.Buffered(1))


def _params(sem):
    return pltpu.CompilerParams(dimension_semantics=sem, vmem_limit_bytes=VMEM_LIMIT)


def _ffn_kernel(h_ref, g_ref, wg_ref, wu_ref, wd_ref, *rest, final_norm):
    if final_norm:
        fg_ref, o_ref, u_ref = rest
    else:
        o_ref, u_ref = rest
    j = pl.program_id(1)

    @pl.when(j == 0)
    def _():
        h = h_ref[...]
        u_ref[...] = _rms(h, g_ref[...]).astype(BF16)
        o_ref[...] = h

    u = u_ref[...]
    a = jnp.dot(u, wg_ref[...], preferred_element_type=F32)
    b = jnp.dot(u, wu_ref[...], preferred_element_type=F32)
    act = (0.5 * _silu(a)) * b
    o_ref[...] += jnp.dot(act.astype(BF16), wd_ref[...], preferred_element_type=F32)

    if final_norm:
        @pl.when(j == pl.num_programs(1) - 1)
        def _():
            o_ref[...] = _rms(o_ref[...], fg_ref[...])


def _ffn(h, g, wg, wu, wd, final_g=None):
    t, d = h.shape
    f = wg.shape[1]
    tm, tf = min(FFN_TM, t), FFN_TF
    assert t % tm == 0 and f % tf == 0
    row = pl.BlockSpec((1, d), lambda i, j: (0, 0))
    in_specs = [
        pl.BlockSpec((tm, d), lambda i, j: (i, 0)),
        row,
        pl.BlockSpec((d, tf), lambda i, j: (0, j)),
        pl.BlockSpec((d, tf), lambda i, j: (0, j)),
        pl.BlockSpec((tf, d), lambda i, j: (j, 0)),
    ]
    args = [h, g, wg, wu, wd]
    if final_g is not None:
        in_specs.append(row)
        args.append(final_g)
    return pl.pallas_call(
        functools.partial(_ffn_kernel, final_norm=final_g is not None),
        grid=(t // tm, f // tf),
        in_specs=in_specs,
        out_specs=pl.BlockSpec((tm, d), lambda i, j: (i, 0)),
        out_shape=jax.ShapeDtypeStruct((t, d), F32),
        scratch_shapes=[pltpu.VMEM((tm, d), BF16)],
        compiler_params=_params(("parallel", "arbitrary")),
        name="ffn_final" if final_g is not None else "ffn",
    )(*args)


def _inproj_kernel(h_ref, g_ref, wp_ref, wq_ref, wdt_ref, p_ref, qkv_ref, dtt_ref):
    u = _rms(h_ref[...], g_ref[...]).astype(BF16)
    p_ref[...] = jnp.dot(u, wp_ref[...], preferred_element_type=F32)
    qkv_ref[...] = jnp.dot(u, wq_ref[...], preferred_element_type=F32).astype(BF16)
    dtt_ref[...] = lax.dot_general(wdt_ref[...], u, (((1,), (1,)), ((), ())), preferred_element_type=F32)


def _inproj(h, g, wp, wq, wdt):
    t, d = h.shape
    npw, nq, heads = wp.shape[1], wq.shape[1], wdt.shape[0]
    tm = min(PROJ_TM, t)
    assert t % tm == 0
    return pl.pallas_call(
        _inproj_kernel,
        grid=(t // tm,),
        in_specs=[
            pl.BlockSpec((tm, d), lambda i: (i, 0)),
            _const_spec((1, d)),
            _const_spec((d, npw)),
            _const_spec((d, nq)),
            _const_spec((heads, d)),
        ],
        out_specs=[
            pl.BlockSpec((tm, npw), lambda i: (i, 0)),
            pl.BlockSpec((tm, nq), lambda i: (i, 0)),
            pl.BlockSpec((heads, tm), lambda i: (0, i)),
        ],
        out_shape=[
            jax.ShapeDtypeStruct((t, npw), F32),
            jax.ShapeDtypeStruct((t, nq), BF16),
            jax.ShapeDtypeStruct((heads, t), F32),
        ],
        compiler_params=_params(("parallel",)),
        name="inproj",
    )(h, g, wp, wq, wdt)


def _pool_kernel(x_ref, w_ref, sc_ref, o_ref, buf_ref, *, ts, group):
    s = pl.program_id(1)
    halo = POOL_HALO

    @pl.when(s == 0)
    def _():
        buf_ref[0:halo, :] = jnp.zeros((halo, buf_ref.shape[1]), F32)

    buf_ref[halo:halo + ts, :] = x_ref[...]
    pos = (s * ts + 1 + lax.broadcasted_iota(jnp.int32, (ts, 1), 0)).astype(F32)
    outs = []
    for i, w in enumerate(POOL_WINDOWS):
        cols = slice(i * group, (i + 1) * group)
        v = buf_ref[halo:halo + ts, cols]
        acc = v
        for k in range(1, w):
            acc = acc + buf_ref[halo - k:halo - k + ts, cols]
        mean = acc / jnp.minimum(pos, float(w))
        outs.append(jnp.dot((mean - v).astype(BF16), w_ref[i], preferred_element_type=F32))
    o_ref[...] = (jnp.concatenate(outs, axis=-1) * sc_ref[...]).astype(BF16)
    buf_ref[0:halo, :] = buf_ref[ts:ts + halo, :]


def _pool(p, col_blk, w_pool, scale, bsz, s_len):
    width = scale.shape[1]
    group = width // len(POOL_WINDOWS)
    ts = min(POOL_TS, s_len)
    ns = s_len // ts
    assert s_len % ts == 0 and ts >= POOL_HALO
    return pl.pallas_call(
        functools.partial(_pool_kernel, ts=ts, group=group),
        grid=(bsz, ns),
        in_specs=[
            pl.BlockSpec((ts, width), lambda b, s: (b * ns + s, col_blk)),
            _const_spec(w_pool.shape),
            _const_spec((1, width)),
        ],
        out_specs=pl.BlockSpec((ts, width), lambda b, s: (b * ns + s, 0)),
        out_shape=jax.ShapeDtypeStruct((bsz * s_len, width), BF16),
        scratch_shapes=[pltpu.VMEM((POOL_HALO + ts, width), F32)],
        compiler_params=_params(("arbitrary", "arbitrary")),
        name="pool",
    )(p, w_pool, scale)


def _ssd_kernel(*refs, chunk, heads):
    L, P, N, G = chunk, SSD_HEAD_DIM, SSD_STATE, SSD_GROUPS
    z_refs, xs_refs = refs[:G], refs[G:2 * G]
    (bc_ref, dt_ref, dtt_ref, cwx_ref, cwb_ref, cbx_ref, cbb_ref, dtb_ref, dtbt_ref, alog_ref, alogt_ref,
     dskip_ref, ng_ref, o_ref, xbuf_ref, bbuf_ref, state_ref) = refs[2 * G:]
    hpg = heads // G
    gw = hpg * P
    halo = CONV_HALO

    @pl.when(pl.program_id(1) == 0)
    def _():
        xbuf_ref[0:halo, :] = jnp.zeros((halo, xbuf_ref.shape[1]), F32)
        bbuf_ref[0:halo, :] = jnp.zeros((halo, bbuf_ref.shape[1]), F32)
        state_ref[...] = jnp.zeros(state_ref.shape, F32)

    for g in range(G):
        xbuf_ref[halo:halo + L, g * gw:(g + 1) * gw] = xs_refs[g][...]
    bbuf_ref[halo:halo + L, :] = bc_ref[...]

    def conv_silu(buf_ref, cw_ref, cb_ref):
        base = halo - (SSD_CONV - 1)
        y = buf_ref[base:base + L, :] * cw_ref[0:1, :]
        for k in range(1, SSD_CONV):
            y = y + buf_ref[base + k:base + k + L, :] * cw_ref[k:k + 1, :]
        return _silu(y + cb_ref[...])

    xs = conv_silu(xbuf_ref, cwx_ref, cbx_ref)
    bc = conv_silu(bbuf_ref, cwb_ref, cbb_ref)
    xbuf_ref[0:halo, :] = xbuf_ref[L:L + halo, :]
    bbuf_ref[0:halo, :] = bbuf_ref[L:L + halo, :]

    head_lane = lax.broadcasted_iota(jnp.int32, (1, LANES), 1) < heads
    dt = jnp.where(head_lane, _softplus(dt_ref[...] + dtb_ref[...]), 0.0)
    a = -jnp.exp(alog_ref[...]) * dt
    a_t = -jnp.exp(alogt_ref[...]) * _softplus(dtt_ref[...] + dtbt_ref[...])

    row = lax.broadcasted_iota(jnp.int32, (L, L), 0)
    col = lax.broadcasted_iota(jnp.int32, (L, L), 1)
    causal = row >= col
    tri = jnp.where(causal, 1.0, 0.0).astype(BF16)
    tri_t = jnp.where(row <= col, 1.0, 0.0).astype(BF16)
    a3, a2, a1 = reversed(_split3(a))
    a_cs = (jnp.dot(tri, a3, preferred_element_type=F32) + jnp.dot(tri, a2, preferred_element_type=F32)
            + jnp.dot(tri, a1, preferred_element_type=F32))
    t3, t2, t1 = reversed(_split3(a_t))
    at_cs = (jnp.dot(t3, tri_t, preferred_element_type=F32) + jnp.dot(t2, tri_t, preferred_element_type=F32)
             + jnp.dot(t1, tri_t, preferred_element_type=F32))

    e_row = lax.broadcasted_iota(jnp.int32, (2 * LANES, heads * P), 0) & (LANES - 1)
    e_col = lax.broadcasted_iota(jnp.int32, (2 * LANES, heads * P), 1) // P
    e2 = jnp.where(e_row == e_col, 1.0, 0.0).astype(BF16)

    def expand(v):
        hi = v.astype(BF16)
        lo = (v - hi.astype(F32)).astype(BF16)
        return jnp.dot(jnp.concatenate([hi, lo], axis=1), e2, preferred_element_type=F32)

    a_last = a_cs[L - 1:L, :]
    dt_e = expand(dt)
    ea_e = expand(jnp.exp(a_cs))
    wd_e = expand(jnp.exp(a_last - a_cs))
    xdt = xs * dt_e
    xdt_b = xdt.astype(BF16)
    xw_b = (xdt * wd_e).astype(BF16)
    first = lax.broadcasted_iota(jnp.int32, (L, 2 * P), 1) < P
    zero = jnp.zeros((L, 2 * P), BF16)

    for g in range(G):
        sl = slice(g * gw, (g + 1) * gw)
        b_g = bc[:, g * N:(g + 1) * N].astype(BF16)
        c_g = bc[:, (G + g) * N:(G + g + 1) * N].astype(BF16)
        cb = lax.dot_general(c_g, b_g, (((1,), (1,)), ((), ())), preferred_element_type=F32)
        st = state_ref[g]
        y_off = jnp.dot(c_g, st.astype(BF16), preferred_element_type=F32) * ea_e[:, sl]
        upd = lax.dot_general(b_g, xw_b[:, sl], (((0,), (0,)), ((), ())), preferred_element_type=F32)
        state_ref[g] = ea_e[L - 1:L, sl] * st + upd
        ys = []
        for e in range(0, hpg, 2):
            h0 = g * hpg + e
            ms = []
            for h in (h0, h0 + 1):
                acol = a_cs[:, h:h + 1]
                arow = at_cs[h:h + 1, :]
                decay = jnp.where(causal, jnp.exp(acol - arow), 0.0)
                ms.append((cb * decay).astype(BF16))
            xp = xdt_b[:, h0 * P:(h0 + 2) * P]
            rhs = jnp.concatenate([jnp.where(first, xp, zero), jnp.where(first, zero, xp)], axis=0)
            ys.append(jnp.dot(jnp.concatenate(ms, axis=1), rhs, preferred_element_type=F32))
        y = jnp.concatenate(ys, axis=1) + y_off + dskip_ref[:, sl] * xs[:, sl]
        y = y * _silu(z_refs[g][...])
        y = y * lax.rsqrt(jnp.mean(y * y, axis=-1, keepdims=True) + RMS_EPS)
        o_ref[:, sl] = (y * ng_ref[:, sl]).astype(BF16)


def _ssd(p, dt_t, off, prm, bsz, s_len):
    heads = dt_t.shape[0]
    width = heads * SSD_HEAD_DIM
    G = SSD_GROUPS
    gw = width // G
    bcw = 2 * G * SSD_STATE
    L = min(SSD_CHUNK, s_len)
    nc = s_len // L
    assert s_len % L == 0 and L >= CONV_HALO and heads % (2 * G) == 0
    assert off["z"] % gw == 0 and off["xs"] % gw == 0 and off["bc"] % bcw == 0 and off["dt"] % LANES == 0

    def tok(w, col):
        return pl.BlockSpec((L, w), lambda b, c: (b * nc + c, col // w))

    in_specs = (
        [tok(gw, off["z"] + g * gw) for g in range(G)] + [tok(gw, off["xs"] + g * gw) for g in range(G)]
        + [tok(bcw, off["bc"]), tok(LANES, off["dt"]),
           pl.BlockSpec((heads, L), lambda b, c: (0, b * nc + c)),
           _const_spec((SSD_CONV, width)), _const_spec((SSD_CONV, bcw)),
           _const_spec((1, width)), _const_spec((1, bcw)),
           _const_spec((1, LANES)), _const_spec((heads, 1)),
           _const_spec((1, LANES)), _const_spec((heads, 1)),
           _const_spec((1, width)), _const_spec((1, width))])
    return pl.pallas_call(
        functools.partial(_ssd_kernel, chunk=L, heads=heads),
        grid=(bsz, nc),
        in_specs=in_specs,
        out_specs=pl.BlockSpec((L, width), lambda b, c: (b * nc + c, 0)),
        out_shape=jax.ShapeDtypeStruct((bsz * s_len, width), BF16),
        scratch_shapes=[
            pltpu.VMEM((CONV_HALO + L, width), F32),
            pltpu.VMEM((CONV_HALO + L, bcw), F32),
            pltpu.VMEM((G, SSD_STATE, gw), F32),
        ],
        compiler_params=_params(("arbitrary", "arbitrary")),
        name="ssd",
    )(*([p] * (2 * G + 2)), dt_t, prm["cwx"], prm["cwb"], prm["cbx"], prm["cbb"], prm["dtb"], prm["dtbt"],
      prm["alog"], prm["alogt"], prm["dskip"], prm["ng"])


def _attn_kernel(q_ref, k_ref, v_ref, o_ref, kbd_ref, vbd_ref, acc_ref, rn_ref, *, tq, nsub):
    sub, hd = ATTN_SUB, ATTN_HEAD_DIM
    i = pl.program_id(2)
    nblk = k_ref.shape[0] // sub

    @pl.when(i == 0)
    def _():
        first = lax.broadcasted_iota(jnp.int32, (sub, 2 * hd), 1) < hd
        zero = jnp.zeros((sub, 2 * hd), BF16)

        def build(n, carry):
            kb = k_ref[pl.ds(n * sub, sub), :]
            vb = v_ref[pl.ds(n * sub, sub), :]
            kbd_ref[n] = jnp.concatenate([jnp.where(first, kb, zero), jnp.where(first, zero, kb)], axis=0)
            vbd_ref[n] = jnp.concatenate([jnp.where(first, vb, zero), jnp.where(first, zero, vb)], axis=0)
            return carry

        lax.fori_loop(0, nblk, build, 0)

    q = q_ref[...]
    rj = lax.broadcasted_iota(jnp.int32, (2 * sub, 2 * sub), 0) & (sub - 1)
    cj = lax.broadcasted_iota(jnp.int32, (2 * sub, 2 * sub), 1)
    tri = (cj < sub) & (rj >= cj)
    uu0 = jnp.where(tri | ((cj >= sub) & (cj < sub + hd)), 1.0, 0.0).astype(BF16)
    uu1 = jnp.where(tri | (cj >= sub + hd), 1.0, 0.0).astype(BF16)
    acc_ref[...] = jnp.zeros(acc_ref.shape, F32)
    rn_ref[...] = jnp.zeros(rn_ref.shape, F32)

    diag_valid = ((lax.broadcasted_iota(jnp.int32, (sub, 2 * sub), 1) & (sub - 1))
                  < lax.broadcasted_iota(jnp.int32, (sub, 2 * sub), 0))

    def mask_top(x, on_diag):
        if not on_diag:
            return x
        top = jnp.where(diag_valid, x[:sub], 0.0)
        return top if x.shape[0] == sub else jnp.concatenate([top, x[sub:]], axis=0)

    def sub_block(n, r0, on_diag):
        z = lax.dot_general(q[r0:], kbd_ref[n], (((1,), (1,)), ((), ())), preferred_element_type=F32)
        sp = jnp.maximum(z, jnp.log(1.0 + jnp.exp2(jnp.minimum(z, EXP2_CLAMP))) * LOG2E)
        sp = mask_top(sp, on_diag)
        hi = sp.astype(BF16)
        lo = (sp - hi.astype(F32)).astype(BF16)
        c0 = jnp.dot(jnp.concatenate([hi[:, :sub], lo[:, :sub]], axis=1), uu0, preferred_element_type=F32)
        c1 = jnp.dot(jnp.concatenate([hi[:, sub:], lo[:, sub:]], axis=1), uu1, preferred_element_type=F32)
        w = mask_top(jnp.exp2(z - jnp.concatenate([c0[:, :sub], c1[:, :sub]], axis=1)), on_diag)
        pv = jnp.dot(w.astype(BF16), vbd_ref[n], preferred_element_type=F32)
        rn = rn_ref[r0:, :]
        acc_ref[r0:, :] = acc_ref[r0:, :] + pv * jnp.exp2(rn)
        rn_ref[r0:, :] = rn - (c0[:, sub:] + c1[:, sub:])

    def chunk(cidx, diagonal):
        for sb in reversed(range(nsub)):
            sub_block(cidx * nsub + sb, sb * sub if diagonal else 0, diagonal)

    chunk(i, True)

    def more(carry):
        t, live = carry
        return (t < i) & (live > 0)

    def body(carry):
        t, _ = carry
        chunk(i - 1 - t, False)
        return t + 1, (jnp.max(rn_ref[...]) > -SKIP_BITS).astype(jnp.int32)

    lax.while_loop(more, body, (jnp.int32(0), jnp.int32(1)))
    o_ref[...] = acc_ref[...].astype(BF16)


def _attn(qkv, bsz, s_len):
    width = qkv.shape[1] // 3
    pair = 2 * ATTN_HEAD_DIM
    npair = width // pair
    tq = min(ATTN_TQ, s_len)
    nq = s_len // tq
    nsub = tq // ATTN_SUB
    assert s_len % tq == 0 and tq % ATTN_SUB == 0 and width % pair == 0 and pair == LANES
    nblk = s_len // ATTN_SUB
    return pl.pallas_call(
        functools.partial(_attn_kernel, tq=tq, nsub=nsub),
        grid=(bsz, npair, nq),
        in_specs=[
            pl.BlockSpec((tq, pair), lambda b, p, i: (b * nq + i, p)),
            pl.BlockSpec((s_len, pair), lambda b, p, i: (b, npair + p)),
            pl.BlockSpec((s_len, pair), lambda b, p, i: (b, 2 * npair + p)),
        ],
        out_specs=pl.BlockSpec((tq, pair), lambda b, p, i: (b * nq + i, p)),
        out_shape=jax.ShapeDtypeStruct((bsz * s_len, width), BF16),
        scratch_shapes=[
            pltpu.VMEM((nblk, 2 * ATTN_SUB, pair), BF16),
            pltpu.VMEM((nblk, 2 * ATTN_SUB, pair), BF16),
            pltpu.VMEM((tq, pair), F32),
            pltpu.VMEM((tq, pair), F32),
        ],
        compiler_params=_params(("arbitrary", "arbitrary", "arbitrary")),
        name="attn",
    )(qkv, qkv, qkv)


def _outproj_kernel(h_ref, p_ref, s_ref, a_ref, wp_ref, ws_ref, wa_ref, o_ref):
    acc = jnp.dot(p_ref[...], wp_ref[...], preferred_element_type=F32)
    acc = acc + jnp.dot(s_ref[...], ws_ref[...], preferred_element_type=F32)
    acc = acc + jnp.dot(a_ref[...], wa_ref[...], preferred_element_type=F32)
    o_ref[...] = h_ref[...] + acc


def _outproj(h, pool_o, ssd_o, attn_o, wp, ws, wa):
    t, d = h.shape
    tm = min(OUT_TM, t)
    assert t % tm == 0

    def tok(w):
        return pl.BlockSpec((tm, w), lambda i: (i, 0))

    return pl.pallas_call(
        _outproj_kernel,
        grid=(t // tm,),
        in_specs=[tok(d), tok(pool_o.shape[1]), tok(ssd_o.shape[1]), tok(attn_o.shape[1]),
                  _const_spec(wp.shape), _const_spec(ws.shape), _const_spec(wa.shape)],
        out_specs=tok(d),
        out_shape=jax.ShapeDtypeStruct((t, d), F32),
        compiler_params=_params(("parallel",)),
        name="outproj",
    )(h, pool_o, ssd_o, attn_o, wp, ws, wa)


def _pad_lanes(v, fill=0.0):
    return jnp.pad(v, (0, LANES - v.shape[0]), constant_values=fill)[None, :]


def kernel(x, ffn1_norm, ffn1_w_gate, ffn1_w_up, ffn1_w_down, mix_norm, w_in, pool_w, pool_scale, conv_w, conv_b, dt_bias, a_log, d_skip, ssd_norm, w_out, ffn2_norm, ffn2_w_gate, ffn2_w_up, ffn2_w_down, final_norm):
    bsz, s_len, d = x.shape
    depth = w_in.shape[0]
    pool_width = pool_scale.shape[1]
    ssd_width = ssd_norm.shape[1]
    heads = dt_bias.shape[1]
    xbc_width = conv_w.shape[2]
    bcw = xbc_width - ssd_width
    attn_width = (w_in.shape[2] - pool_width - ssd_width - xbc_width - heads) // 3
    assert heads <= LANES and heads * SSD_HEAD_DIM == ssd_width and bcw == 2 * SSD_GROUPS * SSD_STATE
    o_z = pool_width
    o_xs = o_z + ssd_width
    o_bc = o_xs + ssd_width
    o_dt = o_bc + bcw
    o_q = o_dt + heads
    slab_width = o_dt + LANES
    assert o_dt % LANES == 0 and slab_width <= w_in.shape[2] and pool_width % LANES == 0
    off = {"z": o_z, "xs": o_xs, "bc": o_bc, "dt": o_dt}
    q_scale = ATTN_HEAD_DIM ** -0.5 * LOG2E

    h = x.reshape(bsz * s_len, d)
    for l in range(depth):
        h = _ffn(h, ffn1_norm[l][None, :], ffn1_w_gate[l].astype(BF16), ffn1_w_up[l].astype(BF16),
                 ffn1_w_down[l].astype(BF16))

        wl = w_in[l]
        wp = wl[:, :slab_width].astype(BF16)
        wq = jnp.concatenate([wl[:, o_q:o_q + attn_width] * q_scale, wl[:, o_q + attn_width:]], axis=1).astype(BF16)
        wdt = wl[:, o_dt:o_dt + heads].T.astype(BF16)
        p, qkv, dt_t = _inproj(h, mix_norm[l][None, :], wp, wq, wdt)

        pool_o = _pool(p, 0, pool_w[l].astype(BF16), pool_scale[l][None, :], bsz, s_len)

        prm = {
            "cwx": conv_w[l][:, :ssd_width], "cwb": conv_w[l][:, ssd_width:],
            "cbx": conv_b[l][None, :ssd_width], "cbb": conv_b[l][None, ssd_width:],
            "dtb": _pad_lanes(dt_bias[l]), "dtbt": dt_bias[l][:, None],
            "alog": _pad_lanes(a_log[l]), "alogt": a_log[l][:, None],
            "dskip": jnp.repeat(d_skip[l], SSD_HEAD_DIM)[None, :], "ng": ssd_norm[l][None, :],
        }
        ssd_o = _ssd(p, dt_t, off, prm, bsz, s_len)

        attn_o = _attn(qkv, bsz, s_len)

        wo = w_out[l].astype(BF16)
        h = _outproj(h, pool_o, ssd_o, attn_o, wo[:pool_width], wo[pool_width:pool_width + ssd_width],
                     wo[pool_width + ssd_width:])

        h = _ffn(h, ffn2_norm[l][None, :], ffn2_w_gate[l].astype(BF16), ffn2_w_up[l].astype(BF16),
                 ffn2_w_down[l].astype(BF16), final_norm[None, :] if l == depth - 1 else None)
    return h.reshape(bsz, s_len, d)
```

```python
import functools

import jax
import jax.numpy as jnp
from jax import lax
from jax.experimental import pallas as pl
from jax.experimental.pallas import tpu as pltpu

F32 = jnp.float32
BF16 = jnp.bfloat16

POOL_WINDOWS = (2, 4, 8, 16)
SSD_HEAD_DIM = 64
SSD_GROUPS = 2
SSD_STATE = 128
SSD_CONV = 4
ATTN_HEAD_DIM = 64
RMS_EPS = 1e-6
LOG2E = 1.4426950408889634
EXP2_CLAMP = 126.0
SKIP_BITS = 160.0

LANES = 128
SUBLANES = 8
VMEM_LIMIT = 60 * 1024 * 1024

FFN_TM = 1024
FFN_TF = 512
PROJ_TM = 256
OUT_TM = 512
POOL_TS = 512
POOL_HALO = 16
SSD_CHUNK = 256
CONV_HALO = 8
ATTN_TQ = 512
ATTN_SUB = 128
ATTN_WALK = 256


def _rms(x, g):
    ms = jnp.mean(x * x, axis=-1, keepdims=True)
    return x * lax.rsqrt(ms + RMS_EPS) * g


def _silu(x):
    h = 0.5 * x
    return h + h * jnp.tanh(h)


def _softplus(x):
    return jnp.maximum(x, 0.0) + jnp.log1p(jnp.exp(-jnp.abs(x)))


def _split3(x):
    x1 = x.astype(BF16)
    r1 = x - x1.astype(F32)
    x2 = r1.astype(BF16)
    x3 = (r1 - x2.astype(F32)).astype(BF16)
    return x1, x2, x3


def _const_spec(shape):
    nd = len(shape)
    return pl.BlockSpec(shape, lambda *_: (0,) * nd, pipeline_mode=pl.Buffered(1))


def _layer_spec(arr, l):
    tail = arr.shape[1:]
    return pl.BlockSpec((None,) + tail, lambda *_: (l,) + (0,) * len(tail), pipeline_mode=pl.Buffered(1))


def _params(sem):
    return pltpu.CompilerParams(dimension_semantics=sem, vmem_limit_bytes=VMEM_LIMIT)


def _ffn_kernel(h_ref, g_ref, wg_ref, wu_ref, wd_ref, *rest, final_norm):
    if final_norm:
        fg_ref, o_ref, u_ref = rest
    else:
        o_ref, u_ref = rest
    j = pl.program_id(1)

    @pl.when(j == 0)
    def _():
        h = h_ref[...]
        u_ref[...] = _rms(h, g_ref[...]).astype(BF16)
        o_ref[...] = h

    u = u_ref[...]
    a = jnp.dot(u, wg_ref[...], preferred_element_type=F32)
    b = jnp.dot(u, wu_ref[...], preferred_element_type=F32)
    act = (0.5 * _silu(a)) * b
    o_ref[...] += jnp.dot(act.astype(BF16), wd_ref[...], preferred_element_type=F32)

    if final_norm:
        @pl.when(j == pl.num_programs(1) - 1)
        def _():
            o_ref[...] = _rms(o_ref[...], fg_ref[...])


def _ffn(h, l, g, wg, wu, wd, final_g=None):
    t, d = h.shape
    f = wg.shape[2]
    tm, tf = min(FFN_TM, t), FFN_TF
    assert t % tm == 0 and f % tf == 0
    in_specs = [
        pl.BlockSpec((tm, d), lambda i, j: (i, 0)),
        _layer_spec(g, l),
        pl.BlockSpec((None, d, tf), lambda i, j: (l, 0, j)),
        pl.BlockSpec((None, d, tf), lambda i, j: (l, 0, j)),
        pl.BlockSpec((None, tf, d), lambda i, j: (l, j, 0)),
    ]
    args = [h, g, wg, wu, wd]
    if final_g is not None:
        in_specs.append(_const_spec(final_g.shape))
        args.append(final_g)
    return pl.pallas_call(
        functools.partial(_ffn_kernel, final_norm=final_g is not None),
        grid=(t // tm, f // tf),
        in_specs=in_specs,
        out_specs=pl.BlockSpec((tm, d), lambda i, j: (i, 0)),
        out_shape=jax.ShapeDtypeStruct((t, d), F32),
        scratch_shapes=[pltpu.VMEM((tm, d), BF16)],
        compiler_params=_params(("parallel", "arbitrary")),
        name="ffn_final" if final_g is not None else "ffn",
    )(*args)


def _inproj_kernel(h_ref, g_ref, w_ref, wdt_ref, p_ref, qkv_ref, dtt_ref):
    npw = p_ref.shape[1]
    u = _rms(h_ref[...], g_ref[...]).astype(BF16)
    p_ref[...] = jnp.dot(u, w_ref[:, :npw], preferred_element_type=F32)
    qkv_ref[...] = jnp.dot(u, w_ref[:, npw:], preferred_element_type=F32).astype(BF16)
    dtt_ref[...] = lax.dot_general(wdt_ref[...], u, (((1,), (1,)), ((), ())), preferred_element_type=F32)


def _inproj(h, l, g, w, wdt, npw):
    t, d = h.shape
    nq, heads = w.shape[2] - npw, wdt.shape[1]
    tm = min(PROJ_TM, t)
    assert t % tm == 0 and npw % LANES == 0 and nq % LANES == 0
    return pl.pallas_call(
        _inproj_kernel,
        grid=(t // tm,),
        in_specs=[
            pl.BlockSpec((tm, d), lambda i: (i, 0)),
            _layer_spec(g, l),
            _layer_spec(w, l),
            _layer_spec(wdt, l),
        ],
        out_specs=[
            pl.BlockSpec((tm, npw), lambda i: (i, 0)),
            pl.BlockSpec((tm, nq), lambda i: (i, 0)),
            pl.BlockSpec((heads, tm), lambda i: (0, i)),
        ],
        out_shape=[
            jax.ShapeDtypeStruct((t, npw), F32),
            jax.ShapeDtypeStruct((t, nq), BF16),
            jax.ShapeDtypeStruct((heads, t), F32),
        ],
        compiler_params=_params(("parallel",)),
        name="inproj",
    )(h, g, w, wdt)


def _pool_kernel(x_ref, w_ref, sc_ref, o_ref, buf_ref, *, ts, group):
    s = pl.program_id(1)
    halo = POOL_HALO

    @pl.when(s == 0)
    def _():
        buf_ref[0:halo, :] = jnp.zeros((halo, buf_ref.shape[1]), F32)

    buf_ref[halo:halo + ts, :] = x_ref[...]
    pos = (s * ts + 1 + lax.broadcasted_iota(jnp.int32, (ts, 1), 0)).astype(F32)
    outs = []
    for i, w in enumerate(POOL_WINDOWS):
        cols = slice(i * group, (i + 1) * group)
        v = buf_ref[halo:halo + ts, cols]
        acc = v
        for k in range(1, w):
            acc = acc + buf_ref[halo - k:halo - k + ts, cols]
        mean = acc / jnp.minimum(pos, float(w))
        outs.append(jnp.dot((mean - v).astype(BF16), w_ref[i], preferred_element_type=F32))
    o_ref[...] = (jnp.concatenate(outs, axis=-1) * sc_ref[...]).astype(BF16)
    buf_ref[0:halo, :] = buf_ref[ts:ts + halo, :]


def _pool(p, col_blk, l, w_pool, scale, bsz, s_len):
    width = scale.shape[2]
    group = width // len(POOL_WINDOWS)
    ts = min(POOL_TS, s_len)
    ns = s_len // ts
    assert s_len % ts == 0 and ts >= POOL_HALO
    return pl.pallas_call(
        functools.partial(_pool_kernel, ts=ts, group=group),
        grid=(bsz, ns),
        in_specs=[
            pl.BlockSpec((ts, width), lambda b, s: (b * ns + s, col_blk)),
            _layer_spec(w_pool, l),
            _layer_spec(scale, l),
        ],
        out_specs=pl.BlockSpec((ts, width), lambda b, s: (b * ns + s, 0)),
        out_shape=jax.ShapeDtypeStruct((bsz * s_len, width), BF16),
        scratch_shapes=[pltpu.VMEM((POOL_HALO + ts, width), F32)],
        compiler_params=_params(("arbitrary", "arbitrary")),
        name="pool",
    )(p, w_pool, scale)


SSD_PARAMS = ("conv_w", "conv_b", "dt_bias", "dt_bias_t", "a_log", "a_log_t", "d_skip", "norm_g")


def _ssd_kernel(*refs, chunk, heads):
    L, P, N, G = chunk, SSD_HEAD_DIM, SSD_STATE, SSD_GROUPS
    z_refs, xs_refs = refs[:G], refs[G:2 * G]
    (bc_ref, dt_ref, dtt_ref, cw_ref, cb_ref, dtb_ref, dtbt_ref, alog_ref, alogt_ref,
     dskip_ref, ng_ref, o_ref, xbuf_ref, bbuf_ref, state_ref) = refs[2 * G:]
    hpg = heads // G
    gw = hpg * P
    halo = CONV_HALO

    @pl.when(pl.program_id(1) == 0)
    def _():
        xbuf_ref[0:halo, :] = jnp.zeros((halo, xbuf_ref.shape[1]), F32)
        bbuf_ref[0:halo, :] = jnp.zeros((halo, bbuf_ref.shape[1]), F32)
        state_ref[...] = jnp.zeros(state_ref.shape, F32)

    for g in range(G):
        xbuf_ref[halo:halo + L, g * gw:(g + 1) * gw] = xs_refs[g][...]
    bbuf_ref[halo:halo + L, :] = bc_ref[...]

    def conv_silu(buf_ref, c0, c1):
        base = halo - (SSD_CONV - 1)
        y = buf_ref[base:base + L, :] * cw_ref[0:1, c0:c1]
        for k in range(1, SSD_CONV):
            y = y + buf_ref[base + k:base + k + L, :] * cw_ref[k:k + 1, c0:c1]
        return _silu(y + cb_ref[:, c0:c1])

    xs = conv_silu(xbuf_ref, 0, heads * P)
    bc = conv_silu(bbuf_ref, heads * P, cw_ref.shape[1])
    xbuf_ref[0:halo, :] = xbuf_ref[L:L + halo, :]
    bbuf_ref[0:halo, :] = bbuf_ref[L:L + halo, :]

    head_lane = lax.broadcasted_iota(jnp.int32, (1, LANES), 1) < heads
    dt = jnp.where(head_lane, _softplus(dt_ref[...] + dtb_ref[...]), 0.0)
    a = -jnp.exp(alog_ref[...]) * dt
    a_t = -jnp.exp(alogt_ref[...]) * _softplus(dtt_ref[...] + dtbt_ref[...])

    row = lax.broadcasted_iota(jnp.int32, (L, L), 0)
    col = lax.broadcasted_iota(jnp.int32, (L, L), 1)
    causal = row >= col
    tri = jnp.where(causal, 1.0, 0.0).astype(BF16)
    tri_t = jnp.where(row <= col, 1.0, 0.0).astype(BF16)
    a3, a2, a1 = reversed(_split3(a))
    a_cs = (jnp.dot(tri, a3, preferred_element_type=F32) + jnp.dot(tri, a2, preferred_element_type=F32)
            + jnp.dot(tri, a1, preferred_element_type=F32))
    t3, t2, t1 = reversed(_split3(a_t))
    at_cs = (jnp.dot(t3, tri_t, preferred_element_type=F32) + jnp.dot(t2, tri_t, preferred_element_type=F32)
             + jnp.dot(t1, tri_t, preferred_element_type=F32))

    e_row = lax.broadcasted_iota(jnp.int32, (2 * LANES, heads * P), 0) & (LANES - 1)
    e_col = lax.broadcasted_iota(jnp.int32, (2 * LANES, heads * P), 1) // P
    e2 = jnp.where(e_row == e_col, 1.0, 0.0).astype(BF16)

    def expand(v):
        hi = v.astype(BF16)
        lo = (v - hi.astype(F32)).astype(BF16)
        return jnp.dot(jnp.concatenate([hi, lo], axis=1), e2, preferred_element_type=F32)

    a_last = a_cs[L - 1:L, :]
    dt_e = expand(dt)
    ea_e = expand(jnp.exp(a_cs))
    wd_e = expand(jnp.exp(a_last - a_cs))
    xdt = xs * dt_e
    xdt_b = xdt.astype(BF16)
    xw_b = (xdt * wd_e).astype(BF16)
    first = lax.broadcasted_iota(jnp.int32, (L, 2 * P), 1) < P
    zero = jnp.zeros((L, 2 * P), BF16)

    for g in range(G):
        sl = slice(g * gw, (g + 1) * gw)
        b_g = bc[:, g * N:(g + 1) * N].astype(BF16)
        c_g = bc[:, (G + g) * N:(G + g + 1) * N].astype(BF16)
        cb = lax.dot_general(c_g, b_g, (((1,), (1,)), ((), ())), preferred_element_type=F32)
        st = state_ref[g]
        y_off = jnp.dot(c_g, st.astype(BF16), preferred_element_type=F32) * ea_e[:, sl]
        upd = lax.dot_general(b_g, xw_b[:, sl], (((0,), (0,)), ((), ())), preferred_element_type=F32)
        state_ref[g] = ea_e[L - 1:L, sl] * st + upd
        ys = []
        for e in range(0, hpg, 2):
            h0 = g * hpg + e
            ms = []
            for h in (h0, h0 + 1):
                acol = a_cs[:, h:h + 1]
                arow = at_cs[h:h + 1, :]
                decay = jnp.where(causal, jnp.exp(acol - arow), 0.0)
                ms.append((cb * decay).astype(BF16))
            xp = xdt_b[:, h0 * P:(h0 + 2) * P]
            rhs = jnp.concatenate([jnp.where(first, xp, zero), jnp.where(first, zero, xp)], axis=0)
            ys.append(jnp.dot(jnp.concatenate(ms, axis=1), rhs, preferred_element_type=F32))
        y = jnp.concatenate(ys, axis=1) + y_off + dskip_ref[:, sl] * xs[:, sl]
        y = y * _silu(z_refs[g][...])
        y = y * lax.rsqrt(jnp.mean(y * y, axis=-1, keepdims=True) + RMS_EPS)
        o_ref[:, sl] = (y * ng_ref[:, sl]).astype(BF16)


def _ssd(p, dt_t, off, l, prm, bsz, s_len):
    heads = dt_t.shape[0]
    width = heads * SSD_HEAD_DIM
    G = SSD_GROUPS
    gw = width // G
    bcw = 2 * G * SSD_STATE
    L = min(SSD_CHUNK, s_len)
    nc = s_len // L
    assert s_len % L == 0 and L >= CONV_HALO and heads % (2 * G) == 0
    assert off["z"] % gw == 0 and off["xs"] % gw == 0 and off["bc"] % bcw == 0 and off["dt"] % LANES == 0

    def tok(w, col):
        return pl.BlockSpec((L, w), lambda b, c: (b * nc + c, col // w))

    in_specs = (
        [tok(gw, off["z"] + g * gw) for g in range(G)] + [tok(gw, off["xs"] + g * gw) for g in range(G)]
        + [tok(bcw, off["bc"]), tok(LANES, off["dt"]),
           pl.BlockSpec((heads, L), lambda b, c: (0, b * nc + c))]
        + [_layer_spec(prm[k], l) for k in SSD_PARAMS])
    return pl.pallas_call(
        functools.partial(_ssd_kernel, chunk=L, heads=heads),
        grid=(bsz, nc),
        in_specs=in_specs,
        out_specs=pl.BlockSpec((L, width), lambda b, c: (b * nc + c, 0)),
        out_shape=jax.ShapeDtypeStruct((bsz * s_len, width), BF16),
        scratch_shapes=[
            pltpu.VMEM((CONV_HALO + L, width), F32),
            pltpu.VMEM((CONV_HALO + L, bcw), F32),
            pltpu.VMEM((G, SSD_STATE, gw), F32),
        ],
        compiler_params=_params(("arbitrary", "arbitrary")),
        name="ssd",
    )(*([p] * (2 * G + 2)), dt_t, *[prm[k] for k in SSD_PARAMS])


def _attn_kernel(q_ref, k_ref, v_ref, o_ref, kbd_ref, vbd_ref, acc_ref, rn_ref, *, tq, nsub, walk):
    sub, hd = ATTN_SUB, ATTN_HEAD_DIM
    i = pl.program_id(2)
    nblk = k_ref.shape[0] // sub

    @pl.when(i == 0)
    def _():
        first = lax.broadcasted_iota(jnp.int32, (sub, 2 * hd), 1) < hd
        zero = jnp.zeros((sub, 2 * hd), BF16)

        def build(n, carry):
            kb = k_ref[pl.ds(n * sub, sub), :]
            vb = v_ref[pl.ds(n * sub, sub), :]
            kbd_ref[n] = jnp.concatenate([jnp.where(first, kb, zero), jnp.where(first, zero, kb)], axis=0)
            vbd_ref[n] = jnp.concatenate([jnp.where(first, vb, zero), jnp.where(first, zero, vb)], axis=0)
            return carry

        lax.fori_loop(0, nblk, build, 0)

    q = q_ref[...]
    rj = lax.broadcasted_iota(jnp.int32, (2 * sub, 2 * sub), 0) & (sub - 1)
    cj = lax.broadcasted_iota(jnp.int32, (2 * sub, 2 * sub), 1)
    tri = (cj < sub) & (rj >= cj)
    uu0 = jnp.where(tri | ((cj >= sub) & (cj < sub + hd)), 1.0, 0.0).astype(BF16)
    uu1 = jnp.where(tri | (cj >= sub + hd), 1.0, 0.0).astype(BF16)
    acc_ref[...] = jnp.zeros(acc_ref.shape, F32)
    rn_ref[...] = jnp.zeros(rn_ref.shape, F32)

    diag_valid = ((lax.broadcasted_iota(jnp.int32, (sub, 2 * sub), 1) & (sub - 1))
                  < lax.broadcasted_iota(jnp.int32, (sub, 2 * sub), 0))

    def mask_top(x, on_diag):
        if not on_diag:
            return x
        top = jnp.where(diag_valid, x[:sub], 0.0)
        return top if x.shape[0] == sub else jnp.concatenate([top, x[sub:]], axis=0)

    def sub_block(n, r0, r1, on_diag):
        z = lax.dot_general(q[r0:r1], kbd_ref[n], (((1,), (1,)), ((), ())), preferred_element_type=F32)
        sp = jnp.maximum(z, jnp.log(1.0 + jnp.exp2(jnp.minimum(z, EXP2_CLAMP))) * LOG2E)
        sp = mask_top(sp, on_diag)
        hi = sp.astype(BF16)
        lo = (sp - hi.astype(F32)).astype(BF16)
        c0 = jnp.dot(jnp.concatenate([hi[:, :sub], lo[:, :sub]], axis=1), uu0, preferred_element_type=F32)
        c1 = jnp.dot(jnp.concatenate([hi[:, sub:], lo[:, sub:]], axis=1), uu1, preferred_element_type=F32)
        w = mask_top(jnp.exp2(z - jnp.concatenate([c0[:, :sub], c1[:, :sub]], axis=1)), on_diag)
        pv = jnp.dot(w.astype(BF16), vbd_ref[n], preferred_element_type=F32)
        rn = rn_ref[r0:r1, :]
        acc_ref[r0:r1, :] = acc_ref[r0:r1, :] + pv * jnp.exp2(rn)
        rn_ref[r0:r1, :] = rn - (c0[:, sub:] + c1[:, sub:])

    for sb in reversed(range(nsub)):
        sub_block(i * nsub + sb, sb * sub, tq, True)

    def live():
        return (jnp.max(rn_ref[...]) > -SKIP_BITS).astype(jnp.int32)

    def more(carry):
        t, alive = carry
        return (t < i * (nsub // walk)) & (alive > 0)

    def body(carry):
        t, _ = carry
        for k in range(walk):
            sub_block(i * nsub - 1 - t * walk - k, 0, tq, False)
        return t + 1, live()

    lax.while_loop(more, body, (jnp.int32(0), live()))
    o_ref[...] = acc_ref[...].astype(BF16)


def _attn(qkv, bsz, s_len):
    width = qkv.shape[1] // 3
    pair = 2 * ATTN_HEAD_DIM
    npair = width // pair
    tq = min(ATTN_TQ, s_len)
    nq = s_len // tq
    nsub = tq // ATTN_SUB
    walk = min(ATTN_WALK, tq) // ATTN_SUB
    assert s_len % tq == 0 and tq % ATTN_SUB == 0 and width % pair == 0 and pair == LANES and nsub % walk == 0
    nblk = s_len // ATTN_SUB
    return pl.pallas_call(
        functools.partial(_attn_kernel, tq=tq, nsub=nsub, walk=walk),
        grid=(bsz, npair, nq),
        in_specs=[
            pl.BlockSpec((tq, pair), lambda b, p, i: (b * nq + i, p)),
            pl.BlockSpec((s_len, pair), lambda b, p, i: (b, npair + p)),
            pl.BlockSpec((s_len, pair), lambda b, p, i: (b, 2 * npair + p)),
        ],
        out_specs=pl.BlockSpec((tq, pair), lambda b, p, i: (b * nq + i, p)),
        out_shape=jax.ShapeDtypeStruct((bsz * s_len, width), BF16),
        scratch_shapes=[
            pltpu.VMEM((nblk, 2 * ATTN_SUB, pair), BF16),
            pltpu.VMEM((nblk, 2 * ATTN_SUB, pair), BF16),
            pltpu.VMEM((tq, pair), F32),
            pltpu.VMEM((tq, pair), F32),
        ],
        compiler_params=_params(("arbitrary", "arbitrary", "arbitrary")),
        name="attn",
    )(qkv, qkv, qkv)


def _outproj_kernel(h_ref, p_ref, s_ref, a_ref, w_ref, o_ref):
    r0 = p_ref.shape[1]
    r1 = r0 + s_ref.shape[1]
    acc = jnp.dot(p_ref[...], w_ref[:r0, :], preferred_element_type=F32)
    acc = acc + jnp.dot(s_ref[...], w_ref[r0:r1, :], preferred_element_type=F32)
    acc = acc + jnp.dot(a_ref[...], w_ref[r1:, :], preferred_element_type=F32)
    o_ref[...] = h_ref[...] + acc


def _outproj(h, pool_o, ssd_o, attn_o, l, w):
    t, d = h.shape
    tm = min(OUT_TM, t)
    assert t % tm == 0 and pool_o.shape[1] + ssd_o.shape[1] + attn_o.shape[1] == w.shape[1]

    def tok(width):
        return pl.BlockSpec((tm, width), lambda i: (i, 0))

    return pl.pallas_call(
        _outproj_kernel,
        grid=(t // tm,),
        in_specs=[tok(d), tok(pool_o.shape[1]), tok(ssd_o.shape[1]), tok(attn_o.shape[1]), _layer_spec(w, l)],
        out_specs=tok(d),
        out_shape=jax.ShapeDtypeStruct((t, d), F32),
        compiler_params=_params(("parallel",)),
        name="outproj",
    )(h, pool_o, ssd_o, attn_o, w)


def _rows(v):
    return v[:, None, :]


def _pad_lanes(v):
    return _rows(jnp.pad(v, ((0, 0), (0, LANES - v.shape[1]))))


def kernel(x, ffn1_norm, ffn1_w_gate, ffn1_w_up, ffn1_w_down, mix_norm, w_in, pool_w, pool_scale, conv_w, conv_b, dt_bias, a_log, d_skip, ssd_norm, w_out, ffn2_norm, ffn2_w_gate, ffn2_w_up, ffn2_w_down, final_norm):
    bsz, s_len, d = x.shape
    depth = w_in.shape[0]
    pool_width = pool_scale.shape[1]
    ssd_width = ssd_norm.shape[1]
    heads = dt_bias.shape[1]
    xbc_width = conv_w.shape[2]
    bcw = xbc_width - ssd_width
    attn_width = (w_in.shape[2] - pool_width - ssd_width - xbc_width - heads) // 3
    assert heads <= LANES and heads * SSD_HEAD_DIM == ssd_width and bcw == 2 * SSD_GROUPS * SSD_STATE
    o_z = pool_width
    o_xs = o_z + ssd_width
    o_bc = o_xs + ssd_width
    o_dt = o_bc + bcw
    o_q = o_dt + heads
    slab_width = o_dt + LANES
    assert o_dt % LANES == 0 and pool_width % LANES == 0
    off = {"z": o_z, "xs": o_xs, "bc": o_bc, "dt": o_dt}
    q_scale = ATTN_HEAD_DIM ** -0.5 * LOG2E

    bf = lambda w: w.astype(BF16)
    w_proj = bf(jnp.concatenate([
        w_in[:, :, :o_q], jnp.zeros((depth, d, slab_width - o_q), w_in.dtype),
        w_in[:, :, o_q:o_q + attn_width] * q_scale, w_in[:, :, o_q + attn_width:]], axis=2))
    w_dt = bf(jnp.swapaxes(w_in[:, :, o_dt:o_dt + heads], 1, 2))
    ffn1 = (_rows(ffn1_norm), bf(ffn1_w_gate), bf(ffn1_w_up), bf(ffn1_w_down))
    ffn2 = (_rows(ffn2_norm), bf(ffn2_w_gate), bf(ffn2_w_up), bf(ffn2_w_down))
    mix_g, w_pool, pool_sc, w_o = _rows(mix_norm), bf(pool_w), _rows(pool_scale), bf(w_out)
    ssd_prm = {
        "conv_w": conv_w, "conv_b": _rows(conv_b),
        "dt_bias": _pad_lanes(dt_bias), "dt_bias_t": dt_bias[:, :, None],
        "a_log": _pad_lanes(a_log), "a_log_t": a_log[:, :, None],
        "d_skip": _rows(jnp.repeat(d_skip, SSD_HEAD_DIM, axis=1)), "norm_g": _rows(ssd_norm),
    }

    h = x.reshape(bsz * s_len, d)
    for l in range(depth):
        h = _ffn(h, l, *ffn1)
        p, qkv, dt_t = _inproj(h, l, mix_g, w_proj, w_dt, slab_width)
        pool_o = _pool(p, 0, l, w_pool, pool_sc, bsz, s_len)
        ssd_o = _ssd(p, dt_t, off, l, ssd_prm, bsz, s_len)
        attn_o = _attn(qkv, bsz, s_len)
        h = _outproj(h, pool_o, ssd_o, attn_o, l, w_o)
        h = _ffn(h, l, *ffn2, final_g=final_norm[None, :] if l == depth - 1 else None)
    return h.reshape(bsz, s_len, d)
```

```python
import functools

import jax
import jax.numpy as jnp
from jax import lax
from jax.experimental import pallas as pl
from jax.experimental.pallas import tpu as pltpu

F32 = jnp.float32
BF16 = jnp.bfloat16

POOL_WINDOWS = (2, 4, 8, 16)
SSD_HEAD_DIM = 64
SSD_GROUPS = 2
SSD_STATE = 128
SSD_CONV = 4
ATTN_HEAD_DIM = 64
RMS_EPS = 1e-6
LOG2E = 1.4426950408889634
EXP2_CLAMP = 126.0
SKIP_BITS = 160.0

LANES = 128
SUBLANES = 8
VMEM_LIMIT = 60 * 1024 * 1024

FFN_TM = 1024
FFN_TF = 512
PROJ_TM = 512
OUT_TM = 512
POOL_HALO = 16
SSD_CHUNK = 256
CONV_HALO = 8
ATTN_TQ = 512
ATTN_SUB = 128
ATTN_WALK = 256


def _rms(x, g):
    ms = jnp.mean(x * x, axis=-1, keepdims=True)
    return x * lax.rsqrt(ms + RMS_EPS) * g


def _silu(x):
    h = 0.5 * x
    return h + h * jnp.tanh(h)


def _softplus(x):
    return jnp.maximum(x, 0.0) + jnp.log1p(jnp.exp(-jnp.abs(x)))


def _split3(x):
    x1 = x.astype(BF16)
    r1 = x - x1.astype(F32)
    x2 = r1.astype(BF16)
    x3 = (r1 - x2.astype(F32)).astype(BF16)
    return x1, x2, x3


def _const_spec(shape):
    nd = len(shape)
    return pl.BlockSpec(shape, lambda *_: (0,) * nd, pipeline_mode=pl.Buffered(1))


def _layer_spec(arr, l):
    tail = arr.shape[1:]
    return pl.BlockSpec((None,) + tail, lambda *_: (l,) + (0,) * len(tail), pipeline_mode=pl.Buffered(1))


def _params(sem):
    return pltpu.CompilerParams(dimension_semantics=sem, vmem_limit_bytes=VMEM_LIMIT)


def _ffn_kernel(h_ref, g_ref, wg_ref, wu_ref, wd_ref, *rest, final_norm, cast_next):
    rest = list(rest)
    fg_ref = rest.pop(0) if final_norm else None
    nxt = [rest.pop(0) for _ in range(3)] if cast_next else []
    o_ref = rest.pop(0)
    cast = [rest.pop(0) for _ in range(3)] if cast_next else []
    (u_ref,) = rest
    j = pl.program_id(1)

    @pl.when(j == 0)
    def _():
        h = h_ref[...]
        u_ref[...] = _rms(h, g_ref[...]).astype(BF16)
        o_ref[...] = h

    u = u_ref[...]
    a = jnp.dot(u, wg_ref[...], preferred_element_type=F32)
    b = jnp.dot(u, wu_ref[...], preferred_element_type=F32)
    act = (0.5 * _silu(a)) * b
    o_ref[...] += jnp.dot(act.astype(BF16), wd_ref[...], preferred_element_type=F32)

    for src_ref, dst_ref in zip(nxt, cast):
        dst_ref[...] = src_ref[...].astype(BF16)

    if final_norm:
        @pl.when(j == pl.num_programs(1) - 1)
        def _():
            o_ref[...] = _rms(o_ref[...], fg_ref[...])


def _ffn_can_cast(t, d):
    ni = t // min(FFN_TM, t)
    return d % ni == 0 and (d // ni) % LANES == 0


def _ffn(h, l, g, wg, wu, wd, final_g=None, nxt=None):
    t, d = h.shape
    f = wg.shape[1]
    tm, tf = min(FFN_TM, t), FFN_TF
    ni = t // tm
    assert t % tm == 0 and f % tf == 0
    in_specs = [
        pl.BlockSpec((tm, d), lambda i, j: (i, 0)),
        _layer_spec(g, l),
        pl.BlockSpec((d, tf), lambda i, j: (0, j)),
        pl.BlockSpec((d, tf), lambda i, j: (0, j)),
        pl.BlockSpec((tf, d), lambda i, j: (j, 0)),
    ]
    args = [h, g, wg, wu, wd]
    out_specs = [pl.BlockSpec((tm, d), lambda i, j: (i, 0))]
    out_shape = [jax.ShapeDtypeStruct((t, d), F32)]
    if final_g is not None:
        in_specs.append(_const_spec(final_g.shape))
        args.append(final_g)
    if nxt is not None:
        assert _ffn_can_cast(t, d)
        ln, rb = nxt[0], d // ni
        in_specs += [pl.BlockSpec((None, rb, tf), lambda i, j: (ln, i, j)),
                     pl.BlockSpec((None, rb, tf), lambda i, j: (ln, i, j)),
                     pl.BlockSpec((None, tf, rb), lambda i, j: (ln, j, i))]
        args += list(nxt[1:])
        out_specs += [pl.BlockSpec((rb, tf), lambda i, j: (i, j)),
                      pl.BlockSpec((rb, tf), lambda i, j: (i, j)),
                      pl.BlockSpec((tf, rb), lambda i, j: (j, i))]
        out_shape += [jax.ShapeDtypeStruct((d, f), BF16), jax.ShapeDtypeStruct((d, f), BF16),
                      jax.ShapeDtypeStruct((f, d), BF16)]
    outs = pl.pallas_call(
        functools.partial(_ffn_kernel, final_norm=final_g is not None, cast_next=nxt is not None),
        grid=(ni, f // tf),
        in_specs=in_specs,
        out_specs=out_specs,
        out_shape=out_shape,
        scratch_shapes=[pltpu.VMEM((tm, d), BF16)],
        compiler_params=_params(("parallel", "arbitrary")),
        name="ffn_final" if final_g is not None else "ffn",
    )(*args)
    return outs[0], tuple(outs[1:])


def _inproj_kernel(h_ref, g_ref, w_ref, p_ref, qkv_ref, dtt_ref, wq_ref, *, q_width, q_scale):
    npw, nq = p_ref.shape[1], qkv_ref.shape[1]
    heads, n_in = dtt_ref.shape[0], w_ref.shape[1]
    o_dt = npw - LANES

    @pl.when(pl.program_id(0) == 0)
    def _():
        r = lax.broadcasted_iota(jnp.int32, (LANES, LANES), 0)
        c = lax.broadcasted_iota(jnp.int32, (LANES, LANES), 1)
        head = jnp.where(r == c + heads, 1.0, 0.0).astype(BF16)
        tail = jnp.where(r + (LANES - heads) == c, 1.0, 0.0).astype(BF16)
        for t in range(nq // LANES):
            lo = o_dt + (t + 1) * LANES
            nxt = w_ref[:, lo:min(lo + LANES, n_in)]
            tile = (jnp.dot(w_ref[:, lo - LANES:lo], head, preferred_element_type=F32)
                    + jnp.dot(nxt, tail[:nxt.shape[1], :], preferred_element_type=F32))
            wq_ref[:, t * LANES:(t + 1) * LANES] = tile.astype(BF16)

    u = _rms(h_ref[...], g_ref[...]).astype(BF16)
    p_ref[...] = jnp.dot(u, w_ref[:, :npw], preferred_element_type=F32)
    col_scale = jnp.where(lax.broadcasted_iota(jnp.int32, (1, nq), 1) < q_width, q_scale, 1.0)
    qkv_ref[...] = (jnp.dot(u, wq_ref[...], preferred_element_type=F32) * col_scale).astype(BF16)
    dt_all = lax.dot_general(w_ref[:, o_dt:npw], u, (((0,), (1,)), ((), ())), preferred_element_type=F32)
    dtt_ref[...] = dt_all[:heads, :]


def _inproj(h, l, g, w, npw, heads, q_scale):
    t, d = h.shape
    nq = w.shape[2] - (npw - LANES) - heads
    tm = min(PROJ_TM, t)
    assert t % tm == 0 and npw % LANES == 0 and nq % (3 * LANES) == 0 and npw <= w.shape[2]
    return pl.pallas_call(
        functools.partial(_inproj_kernel, q_width=nq // 3, q_scale=q_scale),
        grid=(t // tm,),
        in_specs=[
            pl.BlockSpec((tm, d), lambda i: (i, 0)),
            _layer_spec(g, l),
            _layer_spec(w, l),
        ],
        out_specs=[
            pl.BlockSpec((tm, npw), lambda i: (i, 0)),
            pl.BlockSpec((tm, nq), lambda i: (i, 0)),
            pl.BlockSpec((heads, tm), lambda i: (0, i)),
        ],
        out_shape=[
            jax.ShapeDtypeStruct((t, npw), F32),
            jax.ShapeDtypeStruct((t, nq), BF16),
            jax.ShapeDtypeStruct((heads, t), F32),
        ],
        scratch_shapes=[pltpu.VMEM((d, nq), BF16)],
        compiler_params=_params(("arbitrary",)),
        name="inproj",
    )(h, g, w)


def _pool_tile(x_ref, buf_ref, w_ref, sc_ref, seq_tile):
    ts, width = x_ref.shape
    group = width // len(POOL_WINDOWS)
    halo = POOL_HALO

    @pl.when(seq_tile == 0)
    def _():
        buf_ref[0:halo, :] = jnp.zeros((halo, width), F32)

    buf_ref[halo:halo + ts, :] = x_ref[...]
    pos = (seq_tile * ts + 1 + lax.broadcasted_iota(jnp.int32, (ts, 1), 0)).astype(F32)
    outs = []
    for i, w in enumerate(POOL_WINDOWS):
        cols = slice(i * group, (i + 1) * group)
        v = buf_ref[halo:halo + ts, cols]
        acc = v
        for k in range(1, w):
            acc = acc + buf_ref[halo - k:halo - k + ts, cols]
        mean = acc / jnp.minimum(pos, float(w))
        outs.append(jnp.dot((mean - v).astype(BF16), w_ref[i], preferred_element_type=F32))
    buf_ref[0:halo, :] = buf_ref[ts:ts + halo, :]
    return jnp.concatenate(outs, axis=-1) * sc_ref[...]


SSD_PARAMS = ("conv_w", "conv_b", "dt_bias", "dt_bias_t", "a_log", "a_log_t", "d_skip", "norm_g")


def _ssd_kernel(*refs, chunk, heads):
    L, P, N, G = chunk, SSD_HEAD_DIM, SSD_STATE, SSD_GROUPS
    z_refs, xs_refs = refs[:G], refs[G:2 * G]
    (bc_ref, dt_ref, dtt_ref, cw_ref, cb_ref, dtb_ref, dtbt_ref, alog_ref, alogt_ref,
     dskip_ref, ng_ref, o_ref, xbuf_ref, bbuf_ref, state_ref) = refs[2 * G:]
    hpg = heads // G
    gw = hpg * P
    halo = CONV_HALO

    @pl.when(pl.program_id(1) == 0)
    def _():
        xbuf_ref[0:halo, :] = jnp.zeros((halo, xbuf_ref.shape[1]), F32)
        bbuf_ref[0:halo, :] = jnp.zeros((halo, bbuf_ref.shape[1]), F32)
        state_ref[...] = jnp.zeros(state_ref.shape, F32)

    for g in range(G):
        xbuf_ref[halo:halo + L, g * gw:(g + 1) * gw] = xs_refs[g][...]
    bbuf_ref[halo:halo + L, :] = bc_ref[...]

    def conv_silu(buf_ref, c0, c1):
        base = halo - (SSD_CONV - 1)
        y = buf_ref[base:base + L, :] * cw_ref[0:1, c0:c1]
        for k in range(1, SSD_CONV):
            y = y + buf_ref[base + k:base + k + L, :] * cw_ref[k:k + 1, c0:c1]
        return _silu(y + cb_ref[:, c0:c1])

    xs = conv_silu(xbuf_ref, 0, heads * P)
    bc = conv_silu(bbuf_ref, heads * P, cw_ref.shape[1])
    xbuf_ref[0:halo, :] = xbuf_ref[L:L + halo, :]
    bbuf_ref[0:halo, :] = bbuf_ref[L:L + halo, :]

    head_lane = lax.broadcasted_iota(jnp.int32, (1, LANES), 1) < heads
    dt = jnp.where(head_lane, _softplus(dt_ref[...] + dtb_ref[...]), 0.0)
    a = -jnp.exp(alog_ref[...]) * dt
    a_t = -jnp.exp(alogt_ref[...]) * _softplus(dtt_ref[...] + dtbt_ref[...])

    row = lax.broadcasted_iota(jnp.int32, (L, L), 0)
    col = lax.broadcasted_iota(jnp.int32, (L, L), 1)
    causal = row >= col
    tri = jnp.where(causal, 1.0, 0.0).astype(BF16)
    tri_t = jnp.where(row <= col, 1.0, 0.0).astype(BF16)
    a3, a2, a1 = reversed(_split3(a))
    a_cs = (jnp.dot(tri, a3, preferred_element_type=F32) + jnp.dot(tri, a2, preferred_element_type=F32)
            + jnp.dot(tri, a1, preferred_element_type=F32))
    t3, t2, t1 = reversed(_split3(a_t))
    at_cs = (jnp.dot(t3, tri_t, preferred_element_type=F32) + jnp.dot(t2, tri_t, preferred_element_type=F32)
             + jnp.dot(t1, tri_t, preferred_element_type=F32))

    e_row = lax.broadcasted_iota(jnp.int32, (2 * LANES, heads * P), 0) & (LANES - 1)
    e_col = lax.broadcasted_iota(jnp.int32, (2 * LANES, heads * P), 1) // P
    e2 = jnp.where(e_row == e_col, 1.0, 0.0).astype(BF16)

    def expand(v):
        hi = v.astype(BF16)
        lo = (v - hi.astype(F32)).astype(BF16)
        return jnp.dot(jnp.concatenate([hi, lo], axis=1), e2, preferred_element_type=F32)

    a_last = a_cs[L - 1:L, :]
    dt_e = expand(dt)
    ea_e = expand(jnp.exp(a_cs))
    wd_e = expand(jnp.exp(a_last - a_cs))
    xdt = xs * dt_e
    xdt_b = xdt.astype(BF16)
    xw_b = (xdt * wd_e).astype(BF16)
    first = lax.broadcasted_iota(jnp.int32, (L, 2 * P), 1) < P
    zero = jnp.zeros((L, 2 * P), BF16)

    for g in range(G):
        sl = slice(g * gw, (g + 1) * gw)
        b_g = bc[:, g * N:(g + 1) * N].astype(BF16)
        c_g = bc[:, (G + g) * N:(G + g + 1) * N].astype(BF16)
        cb = lax.dot_general(c_g, b_g, (((1,), (1,)), ((), ())), preferred_element_type=F32)
        st = state_ref[g]
        y_off = jnp.dot(c_g, st.astype(BF16), preferred_element_type=F32) * ea_e[:, sl]
        upd = lax.dot_general(b_g, xw_b[:, sl], (((0,), (0,)), ((), ())), preferred_element_type=F32)
        state_ref[g] = ea_e[L - 1:L, sl] * st + upd
        ys = []
        for e in range(0, hpg, 2):
            h0 = g * hpg + e
            ms = []
            for h in (h0, h0 + 1):
                acol = a_cs[:, h:h + 1]
                arow = at_cs[h:h + 1, :]
                decay = jnp.where(causal, jnp.exp(acol - arow), 0.0)
                ms.append((cb * decay).astype(BF16))
            xp = xdt_b[:, h0 * P:(h0 + 2) * P]
            rhs = jnp.concatenate([jnp.where(first, xp, zero), jnp.where(first, zero, xp)], axis=0)
            ys.append(jnp.dot(jnp.concatenate(ms, axis=1), rhs, preferred_element_type=F32))
        y = jnp.concatenate(ys, axis=1) + y_off + dskip_ref[:, sl] * xs[:, sl]
        y = y * _silu(z_refs[g][...])
        y = y * lax.rsqrt(jnp.mean(y * y, axis=-1, keepdims=True) + RMS_EPS)
        o_ref[:, sl] = (y * ng_ref[:, sl]).astype(BF16)


def _ssd(p, dt_t, off, l, prm, bsz, s_len):
    heads = dt_t.shape[0]
    width = heads * SSD_HEAD_DIM
    G = SSD_GROUPS
    gw = width // G
    bcw = 2 * G * SSD_STATE
    L = min(SSD_CHUNK, s_len)
    nc = s_len // L
    assert s_len % L == 0 and L >= CONV_HALO and heads % (2 * G) == 0
    assert off["z"] % gw == 0 and off["xs"] % gw == 0 and off["bc"] % bcw == 0 and off["dt"] % LANES == 0

    def tok(w, col):
        return pl.BlockSpec((L, w), lambda b, c: (b * nc + c, col // w))

    in_specs = (
        [tok(gw, off["z"] + g * gw) for g in range(G)] + [tok(gw, off["xs"] + g * gw) for g in range(G)]
        + [tok(bcw, off["bc"]), tok(LANES, off["dt"]),
           pl.BlockSpec((heads, L), lambda b, c: (0, b * nc + c))]
        + [_layer_spec(prm[k], l) for k in SSD_PARAMS])
    return pl.pallas_call(
        functools.partial(_ssd_kernel, chunk=L, heads=heads),
        grid=(bsz, nc),
        in_specs=in_specs,
        out_specs=pl.BlockSpec((L, width), lambda b, c: (b * nc + c, 0)),
        out_shape=jax.ShapeDtypeStruct((bsz * s_len, width), BF16),
        scratch_shapes=[
            pltpu.VMEM((CONV_HALO + L, width), F32),
            pltpu.VMEM((CONV_HALO + L, bcw), F32),
            pltpu.VMEM((G, SSD_STATE, gw), F32),
        ],
        compiler_params=_params(("arbitrary", "arbitrary")),
        name="ssd",
    )(*([p] * (2 * G + 2)), dt_t, *[prm[k] for k in SSD_PARAMS])


def _attn_kernel(q_ref, k_ref, v_ref, o_ref, kbd_ref, vbd_ref, acc_ref, rn_ref, *, tq, nsub, walk):
    sub, hd = ATTN_SUB, ATTN_HEAD_DIM
    i = pl.program_id(2)
    nblk = k_ref.shape[0] // sub

    @pl.when(i == 0)
    def _():
        first = lax.broadcasted_iota(jnp.int32, (sub, 2 * hd), 1) < hd
        zero = jnp.zeros((sub, 2 * hd), BF16)

        def build(n, carry):
            kb = k_ref[pl.ds(n * sub, sub), :]
            vb = v_ref[pl.ds(n * sub, sub), :]
            kbd_ref[n] = jnp.concatenate([jnp.where(first, kb, zero), jnp.where(first, zero, kb)], axis=0)
            vbd_ref[n] = jnp.concatenate([jnp.where(first, vb, zero), jnp.where(first, zero, vb)], axis=0)
            return carry

        lax.fori_loop(0, nblk, build, 0)

    q = q_ref[...]
    rj = lax.broadcasted_iota(jnp.int32, (2 * sub, 2 * sub), 0) & (sub - 1)
    cj = lax.broadcasted_iota(jnp.int32, (2 * sub, 2 * sub), 1)
    tri = (cj < sub) & (rj >= cj)
    uu0 = jnp.where(tri | ((cj >= sub) & (cj < sub + hd)), 1.0, 0.0).astype(BF16)
    uu1 = jnp.where(tri | (cj >= sub + hd), 1.0, 0.0).astype(BF16)
    acc_ref[...] = jnp.zeros(acc_ref.shape, F32)
    rn_ref[...] = jnp.zeros(rn_ref.shape, F32)

    diag_valid = ((lax.broadcasted_iota(jnp.int32, (sub, 2 * sub), 1) & (sub - 1))
                  < lax.broadcasted_iota(jnp.int32, (sub, 2 * sub), 0))

    def mask_top(x, on_diag):
        if not on_diag:
            return x
        top = jnp.where(diag_valid, x[:sub], 0.0)
        return top if x.shape[0] == sub else jnp.concatenate([top, x[sub:]], axis=0)

    def sub_block(n, r0, r1, on_diag):
        z = lax.dot_general(q[r0:r1], kbd_ref[n], (((1,), (1,)), ((), ())), preferred_element_type=F32)
        sp = jnp.maximum(z, jnp.log(1.0 + jnp.exp2(jnp.minimum(z, EXP2_CLAMP))) * LOG2E)
        sp = mask_top(sp, on_diag)
        hi = sp.astype(BF16)
        lo = (sp - hi.astype(F32)).astype(BF16)
        c0 = jnp.dot(jnp.concatenate([hi[:, :sub], lo[:, :sub]], axis=1), uu0, preferred_element_type=F32)
        c1 = jnp.dot(jnp.concatenate([hi[:, sub:], lo[:, sub:]], axis=1), uu1, preferred_element_type=F32)
        w = mask_top(jnp.exp2(z - jnp.concatenate([c0[:, :sub], c1[:, :sub]], axis=1)), on_diag)
        pv = jnp.dot(w.astype(BF16), vbd_ref[n], preferred_element_type=F32)
        rn = rn_ref[r0:r1, :]
        acc_ref[r0:r1, :] = acc_ref[r0:r1, :] + pv * jnp.exp2(rn)
        rn_ref[r0:r1, :] = rn - (c0[:, sub:] + c1[:, sub:])

    for sb in reversed(range(nsub)):
        sub_block(i * nsub + sb, sb * sub, tq, True)

    def live():
        return (jnp.max(rn_ref[...]) > -SKIP_BITS).astype(jnp.int32)

    def more(carry):
        t, alive = carry
        return (t < i * (nsub // walk)) & (alive > 0)

    def body(carry):
        t, _ = carry
        for k in range(walk):
            sub_block(i * nsub - 1 - t * walk - k, 0, tq, False)
        return t + 1, live()

    lax.while_loop(more, body, (jnp.int32(0), live()))
    o_ref[...] = acc_ref[...].astype(BF16)


def _attn(qkv, bsz, s_len):
    width = qkv.shape[1] // 3
    pair = 2 * ATTN_HEAD_DIM
    npair = width // pair
    tq = min(ATTN_TQ, s_len)
    nq = s_len // tq
    nsub = tq // ATTN_SUB
    walk = min(ATTN_WALK, tq) // ATTN_SUB
    assert s_len % tq == 0 and tq % ATTN_SUB == 0 and width % pair == 0 and pair == LANES and nsub % walk == 0
    nblk = s_len // ATTN_SUB
    return pl.pallas_call(
        functools.partial(_attn_kernel, tq=tq, nsub=nsub, walk=walk),
        grid=(bsz, npair, nq),
        in_specs=[
            pl.BlockSpec((tq, pair), lambda b, p, i: (b * nq + i, p)),
            pl.BlockSpec((s_len, pair), lambda b, p, i: (b, npair + p)),
            pl.BlockSpec((s_len, pair), lambda b, p, i: (b, 2 * npair + p)),
        ],
        out_specs=pl.BlockSpec((tq, pair), lambda b, p, i: (b * nq + i, p)),
        out_shape=jax.ShapeDtypeStruct((bsz * s_len, width), BF16),
        scratch_shapes=[
            pltpu.VMEM((nblk, 2 * ATTN_SUB, pair), BF16),
            pltpu.VMEM((nblk, 2 * ATTN_SUB, pair), BF16),
            pltpu.VMEM((tq, pair), F32),
            pltpu.VMEM((tq, pair), F32),
        ],
        compiler_params=_params(("arbitrary", "arbitrary", "arbitrary")),
        name="attn",
    )(qkv, qkv, qkv)


def _outproj_kernel(h_ref, x_ref, s_ref, a_ref, pw_ref, psc_ref, w_ref, o_ref, buf_ref, *, tiles_per_seq):
    pool_o = _pool_tile(x_ref, buf_ref, pw_ref, psc_ref, pl.program_id(0) % tiles_per_seq)
    r0 = x_ref.shape[1]
    r1 = r0 + s_ref.shape[1]
    acc = jnp.dot(pool_o.astype(BF16), w_ref[:r0, :], preferred_element_type=F32)
    acc = acc + jnp.dot(s_ref[...], w_ref[r0:r1, :], preferred_element_type=F32)
    acc = acc + jnp.dot(a_ref[...], w_ref[r1:, :], preferred_element_type=F32)
    o_ref[...] = h_ref[...] + acc


def _outproj(h, p, pool_width, ssd_o, attn_o, l, w_pool, pool_scale, w, s_len):
    t, d = h.shape
    tm = min(OUT_TM, s_len)
    assert s_len % tm == 0 and tm >= POOL_HALO and pool_width + ssd_o.shape[1] + attn_o.shape[1] == w.shape[1]

    def tok(width):
        return pl.BlockSpec((tm, width), lambda i: (i, 0))

    return pl.pallas_call(
        functools.partial(_outproj_kernel, tiles_per_seq=s_len // tm),
        grid=(t // tm,),
        in_specs=[tok(d), tok(pool_width), tok(ssd_o.shape[1]), tok(attn_o.shape[1]),
                  _layer_spec(w_pool, l), _layer_spec(pool_scale, l), _layer_spec(w, l)],
        out_specs=tok(d),
        out_shape=jax.ShapeDtypeStruct((t, d), F32),
        scratch_shapes=[pltpu.VMEM((POOL_HALO + tm, pool_width), F32)],
        compiler_params=_params(("arbitrary",)),
        name="outproj",
    )(h, p, ssd_o, attn_o, w_pool, pool_scale, w)


def _rows(v):
    return v[:, None, :]


def _pad_lanes(v):
    return _rows(jnp.pad(v, ((0, 0), (0, LANES - v.shape[1]))))


def kernel(x, ffn1_norm, ffn1_w_gate, ffn1_w_up, ffn1_w_down, mix_norm, w_in, pool_w, pool_scale, conv_w, conv_b, dt_bias, a_log, d_skip, ssd_norm, w_out, ffn2_norm, ffn2_w_gate, ffn2_w_up, ffn2_w_down, final_norm):
    bsz, s_len, d = x.shape
    depth = w_in.shape[0]
    pool_width = pool_scale.shape[1]
    ssd_width = ssd_norm.shape[1]
    heads = dt_bias.shape[1]
    xbc_width = conv_w.shape[2]
    bcw = xbc_width - ssd_width
    attn_width = (w_in.shape[2] - pool_width - ssd_width - xbc_width - heads) // 3
    assert heads <= LANES and heads * SSD_HEAD_DIM == ssd_width and bcw == 2 * SSD_GROUPS * SSD_STATE
    o_z = pool_width
    o_xs = o_z + ssd_width
    o_bc = o_xs + ssd_width
    o_dt = o_bc + bcw
    o_q = o_dt + heads
    slab_width = o_dt + LANES
    assert o_dt % LANES == 0 and pool_width % LANES == 0 and o_q + 3 * attn_width == w_in.shape[2]
    off = {"z": o_z, "xs": o_xs, "bc": o_bc, "dt": o_dt}
    q_scale = ATTN_HEAD_DIM ** -0.5 * LOG2E

    bf = lambda w: w.astype(BF16)
    w_proj = bf(w_in)
    ffn_g = (_rows(ffn1_norm), _rows(ffn2_norm))
    ffn_w = ((ffn1_w_gate, ffn1_w_up, ffn1_w_down), (ffn2_w_gate, ffn2_w_up, ffn2_w_down))
    mix_g, w_pool, pool_sc, w_o = _rows(mix_norm), bf(pool_w), _rows(pool_scale), bf(w_out)
    ssd_prm = {
        "conv_w": conv_w, "conv_b": _rows(conv_b),
        "dt_bias": _pad_lanes(dt_bias), "dt_bias_t": dt_bias[:, :, None],
        "a_log": _pad_lanes(a_log), "a_log_t": a_log[:, :, None],
        "d_skip": _rows(jnp.repeat(d_skip, SSD_HEAD_DIM, axis=1)), "norm_g": _rows(ssd_norm),
    }

    h = x.reshape(bsz * s_len, d)
    side_cast = _ffn_can_cast(bsz * s_len, d)
    w_cur = tuple(bf(w[0]) for w in ffn_w[0])

    def ffn(h, l, which, w_cur):
        last = l == depth - 1 and which == 1
        nl, nw = (l, 1) if which == 0 else (l + 1, 0)
        nxt = None if last or not side_cast else (nl,) + ffn_w[nw]
        h, w_next = _ffn(h, l, ffn_g[which], *w_cur, final_g=final_norm[None, :] if last else None, nxt=nxt)
        if not last and not side_cast:
            w_next = tuple(bf(w[nl]) for w in ffn_w[nw])
        return h, w_next

    for l in range(depth):
        h, w_cur = ffn(h, l, 0, w_cur)
        p, qkv, dt_t = _inproj(h, l, mix_g, w_proj, slab_width, heads, q_scale)
        ssd_o = _ssd(p, dt_t, off, l, ssd_prm, bsz, s_len)
        attn_o = _attn(qkv, bsz, s_len)
        h = _outproj(h, p, pool_width, ssd_o, attn_o, l, w_pool, pool_sc, w_o, s_len)
        h, w_cur = ffn(h, l, 1, w_cur)
    return h.reshape(bsz, s_len, d)
```

```python
import functools

import jax
import jax.numpy as jnp
from jax import lax
from jax.experimental import pallas as pl
from jax.experimental.pallas import tpu as pltpu

F32 = jnp.float32
BF16 = jnp.bfloat16

POOL_WINDOWS = (2, 4, 8, 16)
SSD_HEAD_DIM = 64
SSD_GROUPS = 2
SSD_STATE = 128
SSD_CONV = 4
ATTN_HEAD_DIM = 64
RMS_EPS = 1e-6
LOG2E = 1.4426950408889634
EXP2_CLAMP = 126.0
SKIP_BITS = 160.0

LANES = 128
SUBLANES = 8
VMEM_LIMIT = 60 * 1024 * 1024

FFN_TM = 1024
FFN_TF = 512
PROJ_TM = 512
OUT_TM = 512
POOL_HALO = 16
SSD_CHUNK = 256
CONV_HALO = 8
ATTN_TQ = 512
ATTN_SUB = 128
ATTN_WALK = 256


def _rms(x, g):
    ms = jnp.mean(x * x, axis=-1, keepdims=True)
    return x * lax.rsqrt(ms + RMS_EPS) * g


def _silu(x):
    h = 0.5 * x
    return h + h * jnp.tanh(h)


def _softplus(x):
    return jnp.maximum(x, 0.0) + jnp.log1p(jnp.exp(-jnp.abs(x)))


def _split3(x):
    x1 = x.astype(BF16)
    r1 = x - x1.astype(F32)
    x2 = r1.astype(BF16)
    x3 = (r1 - x2.astype(F32)).astype(BF16)
    return x1, x2, x3


def _const_spec(shape):
    nd = len(shape)
    return pl.BlockSpec(shape, lambda *_: (0,) * nd, pipeline_mode=pl.Buffered(1))


def _layer_spec(arr, l):
    tail = arr.shape[1:]
    return pl.BlockSpec((None,) + tail, lambda *_: (l,) + (0,) * len(tail), pipeline_mode=pl.Buffered(1))


def _params(sem):
    return pltpu.CompilerParams(dimension_semantics=sem, vmem_limit_bytes=VMEM_LIMIT)


def _ffn_kernel(h_ref, g_ref, wg_ref, wu_ref, wd_ref, *rest, final_norm, cast_next):
    rest = list(rest)
    fg_ref = rest.pop(0) if final_norm else None
    nxt = [rest.pop(0) for _ in range(3)] if cast_next else []
    o_ref = rest.pop(0)
    cast = [rest.pop(0) for _ in range(3)] if cast_next else []
    (u_ref,) = rest
    j = pl.program_id(1)

    @pl.when(j == 0)
    def _():
        h = h_ref[...]
        u_ref[...] = _rms(h, g_ref[...]).astype(BF16)
        o_ref[...] = h

    u = u_ref[...]
    a = jnp.dot(u, wg_ref[...], preferred_element_type=F32)
    b = jnp.dot(u, wu_ref[...], preferred_element_type=F32)
    act = (0.5 * _silu(a)) * b
    o_ref[...] += jnp.dot(act.astype(BF16), wd_ref[...], preferred_element_type=F32)

    for src_ref, dst_ref in zip(nxt, cast):
        dst_ref[...] = src_ref[...].astype(BF16)

    if final_norm:
        @pl.when(j == pl.num_programs(1) - 1)
        def _():
            o_ref[...] = _rms(o_ref[...], fg_ref[...])


def _ffn_can_cast(t, d):
    ni = t // min(FFN_TM, t)
    return d % ni == 0 and (d // ni) % LANES == 0


def _ffn(h, l, g, wg, wu, wd, final_g=None, nxt=None):
    t, d = h.shape
    f = wg.shape[1]
    tm, tf = min(FFN_TM, t), FFN_TF
    ni = t // tm
    assert t % tm == 0 and f % tf == 0
    in_specs = [
        pl.BlockSpec((tm, d), lambda i, j: (i, 0)),
        _layer_spec(g, l),
        pl.BlockSpec((d, tf), lambda i, j: (0, j)),
        pl.BlockSpec((d, tf), lambda i, j: (0, j)),
        pl.BlockSpec((tf, d), lambda i, j: (j, 0)),
    ]
    args = [h, g, wg, wu, wd]
    out_specs = [pl.BlockSpec((tm, d), lambda i, j: (i, 0))]
    out_shape = [jax.ShapeDtypeStruct((t, d), F32)]
    if final_g is not None:
        in_specs.append(_const_spec(final_g.shape))
        args.append(final_g)
    if nxt is not None:
        assert _ffn_can_cast(t, d)
        ln, rb = nxt[0], d // ni
        in_specs += [pl.BlockSpec((None, rb, tf), lambda i, j: (ln, i, j)),
                     pl.BlockSpec((None, rb, tf), lambda i, j: (ln, i, j)),
                     pl.BlockSpec((None, tf, rb), lambda i, j: (ln, j, i))]
        args += list(nxt[1:])
        out_specs += [pl.BlockSpec((rb, tf), lambda i, j: (i, j)),
                      pl.BlockSpec((rb, tf), lambda i, j: (i, j)),
                      pl.BlockSpec((tf, rb), lambda i, j: (j, i))]
        out_shape += [jax.ShapeDtypeStruct((d, f), BF16), jax.ShapeDtypeStruct((d, f), BF16),
                      jax.ShapeDtypeStruct((f, d), BF16)]
    outs = pl.pallas_call(
        functools.partial(_ffn_kernel, final_norm=final_g is not None, cast_next=nxt is not None),
        grid=(ni, f // tf),
        in_specs=in_specs,
        out_specs=out_specs,
        out_shape=out_shape,
        scratch_shapes=[pltpu.VMEM((tm, d), BF16)],
        compiler_params=_params(("parallel", "arbitrary")),
        name="ffn_final" if final_g is not None else "ffn",
    )(*args)
    return outs[0], tuple(outs[1:])


def _inproj_kernel(h_ref, g_ref, w_ref, p_ref, qkv_ref, dtt_ref, wq_ref, *, q_width, q_scale):
    npw, nq = p_ref.shape[1], qkv_ref.shape[1]
    heads, n_in = dtt_ref.shape[0], w_ref.shape[1]
    o_dt = npw - LANES

    @pl.when(pl.program_id(0) == 0)
    def _():
        r = lax.broadcasted_iota(jnp.int32, (LANES, LANES), 0)
        c = lax.broadcasted_iota(jnp.int32, (LANES, LANES), 1)
        head = jnp.where(r == c + heads, 1.0, 0.0).astype(BF16)
        tail = jnp.where(r + (LANES - heads) == c, 1.0, 0.0).astype(BF16)
        for t in range(nq // LANES):
            lo = o_dt + (t + 1) * LANES
            nxt = w_ref[:, lo:min(lo + LANES, n_in)]
            tile = (jnp.dot(w_ref[:, lo - LANES:lo], head, preferred_element_type=F32)
                    + jnp.dot(nxt, tail[:nxt.shape[1], :], preferred_element_type=F32))
            wq_ref[:, t * LANES:(t + 1) * LANES] = tile.astype(BF16)

    u = _rms(h_ref[...], g_ref[...]).astype(BF16)
    p_ref[...] = jnp.dot(u, w_ref[:, :npw], preferred_element_type=F32)
    col_scale = jnp.where(lax.broadcasted_iota(jnp.int32, (1, nq), 1) < q_width, q_scale, 1.0)
    qkv_ref[...] = (jnp.dot(u, wq_ref[...], preferred_element_type=F32) * col_scale).astype(BF16)
    dt_all = lax.dot_general(w_ref[:, o_dt:npw], u, (((0,), (1,)), ((), ())), preferred_element_type=F32)
    dtt_ref[...] = dt_all[:heads, :]


def _inproj(h, l, g, w, npw, heads, q_scale):
    t, d = h.shape
    nq = w.shape[2] - (npw - LANES) - heads
    tm = min(PROJ_TM, t)
    assert t % tm == 0 and npw % LANES == 0 and nq % (3 * LANES) == 0 and npw <= w.shape[2]
    return pl.pallas_call(
        functools.partial(_inproj_kernel, q_width=nq // 3, q_scale=q_scale),
        grid=(t // tm,),
        in_specs=[
            pl.BlockSpec((tm, d), lambda i: (i, 0)),
            _layer_spec(g, l),
            _layer_spec(w, l),
        ],
        out_specs=[
            pl.BlockSpec((tm, npw), lambda i: (i, 0)),
            pl.BlockSpec((tm, nq), lambda i: (i, 0)),
            pl.BlockSpec((heads, tm), lambda i: (0, i)),
        ],
        out_shape=[
            jax.ShapeDtypeStruct((t, npw), F32),
            jax.ShapeDtypeStruct((t, nq), BF16),
            jax.ShapeDtypeStruct((heads, t), F32),
        ],
        scratch_shapes=[pltpu.VMEM((d, nq), BF16)],
        compiler_params=_params(("arbitrary",)),
        name="inproj",
    )(h, g, w)


def _pool_reset(buf_ref, seq_tile):
    @pl.when(seq_tile == 0)
    def _():
        buf_ref[0:POOL_HALO, :] = jnp.zeros((POOL_HALO, buf_ref.shape[1]), F32)


def _pool_tile(x_ref, buf_ref, w_ref, sc_ref, seq_tile):
    ts, width = x_ref.shape
    group = width // len(POOL_WINDOWS)
    halo = POOL_HALO
    buf_ref[halo:halo + ts, :] = x_ref[...]
    pos = (seq_tile * ts + 1 + lax.broadcasted_iota(jnp.int32, (ts, 1), 0)).astype(F32)
    outs = []
    for i, w in enumerate(POOL_WINDOWS):
        cols = slice(i * group, (i + 1) * group)
        v = buf_ref[halo:halo + ts, cols]
        acc = v
        for k in range(1, w):
            acc = acc + buf_ref[halo - k:halo - k + ts, cols]
        mean = acc / jnp.minimum(pos, float(w))
        outs.append(jnp.dot((mean - v).astype(BF16), w_ref[i], preferred_element_type=F32))
    buf_ref[0:halo, :] = buf_ref[ts:ts + halo, :]
    return jnp.concatenate(outs, axis=-1) * sc_ref[...]


SSD_PARAMS = ("conv_w", "conv_b", "dt_bias", "dt_bias_t", "a_log", "a_log_t", "d_skip", "norm_g")


def _ssd_kernel(*refs, chunk, heads):
    L, P, N, G = chunk, SSD_HEAD_DIM, SSD_STATE, SSD_GROUPS
    z_refs, xs_refs = refs[:G], refs[G:2 * G]
    (bc_ref, dt_ref, dtt_ref, cw_ref, cb_ref, dtb_ref, dtbt_ref, alog_ref, alogt_ref,
     dskip_ref, ng_ref, o_ref, xbuf_ref, bbuf_ref, state_ref) = refs[2 * G:]
    hpg = heads // G
    gw = hpg * P
    halo = CONV_HALO

    @pl.when(pl.program_id(1) == 0)
    def _():
        xbuf_ref[0:halo, :] = jnp.zeros((halo, xbuf_ref.shape[1]), F32)
        bbuf_ref[0:halo, :] = jnp.zeros((halo, bbuf_ref.shape[1]), F32)
        state_ref[...] = jnp.zeros(state_ref.shape, F32)

    for g in range(G):
        xbuf_ref[halo:halo + L, g * gw:(g + 1) * gw] = xs_refs[g][...]
    bbuf_ref[halo:halo + L, :] = bc_ref[...]

    def conv_silu(buf_ref, c0, c1):
        x = buf_ref[...]
        tiles = x.reshape((halo + L) // SUBLANES, SUBLANES, x.shape[1])
        sub = lax.broadcasted_iota(jnp.int32, (1, SUBLANES, 1), 1)
        t0 = halo // SUBLANES
        y = x[halo:, :] * cw_ref[SSD_CONV - 1:SSD_CONV, c0:c1]
        for s in range(1, SSD_CONV):
            r = pltpu.roll(tiles, s, axis=1)
            shifted = jnp.where(sub < s, r[t0 - 1:-1], r[t0:]).reshape(L, x.shape[1])
            y = y + shifted * cw_ref[SSD_CONV - 1 - s:SSD_CONV - s, c0:c1]
        return _silu(y + cb_ref[:, c0:c1])

    xs = conv_silu(xbuf_ref, 0, heads * P)
    bc = conv_silu(bbuf_ref, heads * P, cw_ref.shape[1])
    xbuf_ref[0:halo, :] = xbuf_ref[L:L + halo, :]
    bbuf_ref[0:halo, :] = bbuf_ref[L:L + halo, :]

    head_lane = lax.broadcasted_iota(jnp.int32, (1, LANES), 1) < heads
    dt = jnp.where(head_lane, _softplus(dt_ref[...] + dtb_ref[...]), 0.0)
    a = (-LOG2E * jnp.exp(alog_ref[...])) * dt
    a_t = (-LOG2E * jnp.exp(alogt_ref[...])) * _softplus(dtt_ref[...] + dtbt_ref[...])

    row = lax.broadcasted_iota(jnp.int32, (L, L), 0)
    col = lax.broadcasted_iota(jnp.int32, (L, L), 1)
    causal = row >= col
    tri = jnp.where(causal, 1.0, 0.0).astype(BF16)
    tri_t = jnp.where(row <= col, 1.0, 0.0).astype(BF16)
    a3, a2, a1 = reversed(_split3(a))
    a_cs = (jnp.dot(tri, a3, preferred_element_type=F32) + jnp.dot(tri, a2, preferred_element_type=F32)
            + jnp.dot(tri, a1, preferred_element_type=F32))
    t3, t2, t1 = reversed(_split3(a_t))
    at_cs = (jnp.dot(t3, tri_t, preferred_element_type=F32) + jnp.dot(t2, tri_t, preferred_element_type=F32)
             + jnp.dot(t1, tri_t, preferred_element_type=F32))

    e_row = lax.broadcasted_iota(jnp.int32, (2 * LANES, heads * P), 0) & (LANES - 1)
    e_col = lax.broadcasted_iota(jnp.int32, (2 * LANES, heads * P), 1) // P
    e2 = jnp.where(e_row == e_col, 1.0, 0.0).astype(BF16)

    def expand(v):
        hi = v.astype(BF16)
        lo = (v - hi.astype(F32)).astype(BF16)
        return jnp.dot(jnp.concatenate([hi, lo], axis=1), e2, preferred_element_type=F32)

    a_last = a_cs[L - 1:L, :]
    dt_e = expand(dt)
    ea_e = expand(jnp.exp2(a_cs))
    wd_e = expand(jnp.exp2(a_last - a_cs))
    xdt = xs * dt_e
    xdt_b = xdt.astype(BF16)
    xw_b = (xdt * wd_e).astype(BF16)
    first = lax.broadcasted_iota(jnp.int32, (L, 2 * P), 1) < P
    zero = jnp.zeros((L, 2 * P), BF16)

    for g in range(G):
        sl = slice(g * gw, (g + 1) * gw)
        b_g = bc[:, g * N:(g + 1) * N].astype(BF16)
        c_g = bc[:, (G + g) * N:(G + g + 1) * N].astype(BF16)
        cb = lax.dot_general(c_g, b_g, (((1,), (1,)), ((), ())), preferred_element_type=F32)
        st = state_ref[g]
        y_off = jnp.dot(c_g, st.astype(BF16), preferred_element_type=F32) * ea_e[:, sl]
        upd = lax.dot_general(b_g, xw_b[:, sl], (((0,), (0,)), ((), ())), preferred_element_type=F32)
        state_ref[g] = ea_e[L - 1:L, sl] * st + upd
        ys = []
        for e in range(0, hpg, 2):
            h0 = g * hpg + e
            ms = []
            for h in (h0, h0 + 1):
                acol = a_cs[:, h:h + 1]
                arow = at_cs[h:h + 1, :]
                decay = jnp.where(causal, jnp.exp2(acol - arow), 0.0)
                ms.append((cb * decay).astype(BF16))
            xp = xdt_b[:, h0 * P:(h0 + 2) * P]
            rhs = jnp.concatenate([jnp.where(first, xp, zero), jnp.where(first, zero, xp)], axis=0)
            ys.append(jnp.dot(jnp.concatenate(ms, axis=1), rhs, preferred_element_type=F32))
        y = jnp.concatenate(ys, axis=1) + y_off + dskip_ref[:, sl] * xs[:, sl]
        y = y * _silu(z_refs[g][...])
        y = y * lax.rsqrt(jnp.mean(y * y, axis=-1, keepdims=True) + RMS_EPS)
        o_ref[:, sl] = (y * ng_ref[:, sl]).astype(BF16)


def _ssd(p, dt_t, off, l, prm, bsz, s_len):
    heads = dt_t.shape[0]
    width = heads * SSD_HEAD_DIM
    G = SSD_GROUPS
    gw = width // G
    bcw = 2 * G * SSD_STATE
    L = min(SSD_CHUNK, s_len)
    nc = s_len // L
    assert s_len % L == 0 and L >= CONV_HALO and heads % (2 * G) == 0
    assert off["z"] % gw == 0 and off["xs"] % gw == 0 and off["bc"] % bcw == 0 and off["dt"] % LANES == 0

    def tok(w, col):
        return pl.BlockSpec((L, w), lambda b, c: (b * nc + c, col // w))

    in_specs = (
        [tok(gw, off["z"] + g * gw) for g in range(G)] + [tok(gw, off["xs"] + g * gw) for g in range(G)]
        + [tok(bcw, off["bc"]), tok(LANES, off["dt"]),
           pl.BlockSpec((heads, L), lambda b, c: (0, b * nc + c))]
        + [_layer_spec(prm[k], l) for k in SSD_PARAMS])
    return pl.pallas_call(
        functools.partial(_ssd_kernel, chunk=L, heads=heads),
        grid=(bsz, nc),
        in_specs=in_specs,
        out_specs=pl.BlockSpec((L, width), lambda b, c: (b * nc + c, 0)),
        out_shape=jax.ShapeDtypeStruct((bsz * s_len, width), BF16),
        scratch_shapes=[
            pltpu.VMEM((CONV_HALO + L, width), F32),
            pltpu.VMEM((CONV_HALO + L, bcw), F32),
            pltpu.VMEM((G, SSD_STATE, gw), F32),
        ],
        compiler_params=_params(("arbitrary", "arbitrary")),
        name="ssd",
    )(*([p] * (2 * G + 2)), dt_t, *[prm[k] for k in SSD_PARAMS])


def _attn_kernel(q_ref, k_ref, v_ref, o_ref, kbd_ref, vbd_ref, acc_ref, rn_ref, *, tq, nsub, walk):
    sub, hd = ATTN_SUB, ATTN_HEAD_DIM
    i = pl.program_id(2)
    nblk = k_ref.shape[0] // sub

    @pl.when(i == 0)
    def _():
        first = lax.broadcasted_iota(jnp.int32, (sub, 2 * hd), 1) < hd
        zero = jnp.zeros((sub, 2 * hd), BF16)

        def build(n, carry):
            kb = k_ref[pl.ds(n * sub, sub), :]
            vb = v_ref[pl.ds(n * sub, sub), :]
            kbd_ref[n] = jnp.concatenate([jnp.where(first, kb, zero), jnp.where(first, zero, kb)], axis=0)
            vbd_ref[n] = jnp.concatenate([jnp.where(first, vb, zero), jnp.where(first, zero, vb)], axis=0)
            return carry

        lax.fori_loop(0, nblk, build, 0)

    q = q_ref[...]
    rj = lax.broadcasted_iota(jnp.int32, (2 * sub, 2 * sub), 0) & (sub - 1)
    cj = lax.broadcasted_iota(jnp.int32, (2 * sub, 2 * sub), 1)
    tri = (cj < sub) & (rj >= cj)
    uu0 = jnp.where(tri | ((cj >= sub) & (cj < sub + hd)), 1.0, 0.0).astype(BF16)
    uu1 = jnp.where(tri | (cj >= sub + hd), 1.0, 0.0).astype(BF16)
    acc_ref[...] = jnp.zeros(acc_ref.shape, F32)
    rn_ref[...] = jnp.zeros(rn_ref.shape, F32)

    diag_valid = ((lax.broadcasted_iota(jnp.int32, (sub, 2 * sub), 1) & (sub - 1))
                  < lax.broadcasted_iota(jnp.int32, (sub, 2 * sub), 0))

    def mask_top(x, on_diag):
        if not on_diag:
            return x
        top = jnp.where(diag_valid, x[:sub], 0.0)
        return top if x.shape[0] == sub else jnp.concatenate([top, x[sub:]], axis=0)

    def sub_block(n, r0, r1, on_diag):
        z = lax.dot_general(q[r0:r1], kbd_ref[n], (((1,), (1,)), ((), ())), preferred_element_type=F32)
        sp = jnp.maximum(z, jnp.log(1.0 + jnp.exp2(jnp.minimum(z, EXP2_CLAMP))) * LOG2E)
        sp = mask_top(sp, on_diag)
        hi = sp.astype(BF16)
        lo = (sp - hi.astype(F32)).astype(BF16)
        c0 = jnp.dot(jnp.concatenate([hi[:, :sub], lo[:, :sub]], axis=1), uu0, preferred_element_type=F32)
        c1 = jnp.dot(jnp.concatenate([hi[:, sub:], lo[:, sub:]], axis=1), uu1, preferred_element_type=F32)
        w = mask_top(jnp.exp2(z - jnp.concatenate([c0[:, :sub], c1[:, :sub]], axis=1)), on_diag)
        pv = jnp.dot(w.astype(BF16), vbd_ref[n], preferred_element_type=F32)
        rn = rn_ref[r0:r1, :]
        acc_ref[r0:r1, :] = acc_ref[r0:r1, :] + pv * jnp.exp2(rn)
        rn_ref[r0:r1, :] = rn - (c0[:, sub:] + c1[:, sub:])

    for sb in reversed(range(nsub)):
        sub_block(i * nsub + sb, sb * sub, tq, True)

    def live():
        return (jnp.max(rn_ref[...]) > -SKIP_BITS).astype(jnp.int32)

    def more(carry):
        t, alive = carry
        return (t < i * (nsub // walk)) & (alive > 0)

    def body(carry):
        t, _ = carry
        for k in range(walk):
            sub_block(i * nsub - 1 - t * walk - k, 0, tq, False)
        return t + 1, live()

    lax.while_loop(more, body, (jnp.int32(0), live()))
    o_ref[...] = acc_ref[...].astype(BF16)


def _attn(qkv, bsz, s_len):
    width = qkv.shape[1] // 3
    pair = 2 * ATTN_HEAD_DIM
    npair = width // pair
    tq = min(ATTN_TQ, s_len)
    nq = s_len // tq
    nsub = tq // ATTN_SUB
    walk = min(ATTN_WALK, tq) // ATTN_SUB
    assert s_len % tq == 0 and tq % ATTN_SUB == 0 and width % pair == 0 and pair == LANES and nsub % walk == 0
    nblk = s_len // ATTN_SUB
    return pl.pallas_call(
        functools.partial(_attn_kernel, tq=tq, nsub=nsub, walk=walk),
        grid=(bsz, npair, nq),
        in_specs=[
            pl.BlockSpec((tq, pair), lambda b, p, i: (b * nq + i, p)),
            pl.BlockSpec((s_len, pair), lambda b, p, i: (b, npair + p)),
            pl.BlockSpec((s_len, pair), lambda b, p, i: (b, 2 * npair + p)),
        ],
        out_specs=pl.BlockSpec((tq, pair), lambda b, p, i: (b * nq + i, p)),
        out_shape=jax.ShapeDtypeStruct((bsz * s_len, width), BF16),
        scratch_shapes=[
            pltpu.VMEM((nblk, 2 * ATTN_SUB, pair), BF16),
            pltpu.VMEM((nblk, 2 * ATTN_SUB, pair), BF16),
            pltpu.VMEM((tq, pair), F32),
            pltpu.VMEM((tq, pair), F32),
        ],
        compiler_params=_params(("arbitrary", "arbitrary", "arbitrary")),
        name="attn",
    )(qkv, qkv, qkv)


def _outproj_kernel(h_ref, x_ref, s_ref, a_ref, pw_ref, psc_ref, w_ref, o_ref, buf_ref, *, tiles_per_seq):
    seq_tile = pl.program_id(0) % tiles_per_seq
    _pool_reset(buf_ref, seq_tile)
    r0 = x_ref.shape[1]
    r1 = r0 + s_ref.shape[1]
    acc = jnp.dot(s_ref[...], w_ref[r0:r1, :], preferred_element_type=F32)
    acc = acc + jnp.dot(a_ref[...], w_ref[r1:, :], preferred_element_type=F32)
    pool_o = _pool_tile(x_ref, buf_ref, pw_ref, psc_ref, seq_tile)
    acc = acc + jnp.dot(pool_o.astype(BF16), w_ref[:r0, :], preferred_element_type=F32)
    o_ref[...] = h_ref[...] + acc


def _outproj(h, p, pool_width, ssd_o, attn_o, l, w_pool, pool_scale, w, s_len):
    t, d = h.shape
    tm = min(OUT_TM, s_len)
    assert s_len % tm == 0 and tm >= POOL_HALO and pool_width + ssd_o.shape[1] + attn_o.shape[1] == w.shape[1]

    def tok(width):
        return pl.BlockSpec((tm, width), lambda i: (i, 0))

    return pl.pallas_call(
        functools.partial(_outproj_kernel, tiles_per_seq=s_len // tm),
        grid=(t // tm,),
        in_specs=[tok(d), tok(pool_width), tok(ssd_o.shape[1]), tok(attn_o.shape[1]),
                  _layer_spec(w_pool, l), _layer_spec(pool_scale, l), _layer_spec(w, l)],
        out_specs=tok(d),
        out_shape=jax.ShapeDtypeStruct((t, d), F32),
        scratch_shapes=[pltpu.VMEM((POOL_HALO + tm, pool_width), F32)],
        compiler_params=_params(("arbitrary",)),
        name="outproj",
    )(h, p, ssd_o, attn_o, w_pool, pool_scale, w)


def _rows(v):
    return v[:, None, :]


def _pad_lanes(v):
    return _rows(jnp.pad(v, ((0, 0), (0, LANES - v.shape[1]))))


def kernel(x, ffn1_norm, ffn1_w_gate, ffn1_w_up, ffn1_w_down, mix_norm, w_in, pool_w, pool_scale, conv_w, conv_b, dt_bias, a_log, d_skip, ssd_norm, w_out, ffn2_norm, ffn2_w_gate, ffn2_w_up, ffn2_w_down, final_norm):
    bsz, s_len, d = x.shape
    depth = w_in.shape[0]
    pool_width = pool_scale.shape[1]
    ssd_width = ssd_norm.shape[1]
    heads = dt_bias.shape[1]
    xbc_width = conv_w.shape[2]
    bcw = xbc_width - ssd_width
    attn_width = (w_in.shape[2] - pool_width - ssd_width - xbc_width - heads) // 3
    assert heads <= LANES and heads * SSD_HEAD_DIM == ssd_width and bcw == 2 * SSD_GROUPS * SSD_STATE
    o_z = pool_width
    o_xs = o_z + ssd_width
    o_bc = o_xs + ssd_width
    o_dt = o_bc + bcw
    o_q = o_dt + heads
    slab_width = o_dt + LANES
    assert o_dt % LANES == 0 and pool_width % LANES == 0 and o_q + 3 * attn_width == w_in.shape[2]
    off = {"z": o_z, "xs": o_xs, "bc": o_bc, "dt": o_dt}
    q_scale = ATTN_HEAD_DIM ** -0.5 * LOG2E

    bf = lambda w: w.astype(BF16)
    w_proj = bf(w_in)
    ffn_g = (_rows(ffn1_norm), _rows(ffn2_norm))
    ffn_w = ((ffn1_w_gate, ffn1_w_up, ffn1_w_down), (ffn2_w_gate, ffn2_w_up, ffn2_w_down))
    mix_g, w_pool, pool_sc, w_o = _rows(mix_norm), bf(pool_w), _rows(pool_scale), bf(w_out)
    ssd_prm = {
        "conv_w": conv_w, "conv_b": _rows(conv_b),
        "dt_bias": _pad_lanes(dt_bias), "dt_bias_t": dt_bias[:, :, None],
        "a_log": _pad_lanes(a_log), "a_log_t": a_log[:, :, None],
        "d_skip": _rows(jnp.repeat(d_skip, SSD_HEAD_DIM, axis=1)), "norm_g": _rows(ssd_norm),
    }

    h = x.reshape(bsz * s_len, d)
    side_cast = _ffn_can_cast(bsz * s_len, d)
    w_cur = tuple(bf(w[0]) for w in ffn_w[0])

    def ffn(h, l, which, w_cur):
        last = l == depth - 1 and which == 1
        nl, nw = (l, 1) if which == 0 else (l + 1, 0)
        nxt = None if last or not side_cast else (nl,) + ffn_w[nw]
        h, w_next = _ffn(h, l, ffn_g[which], *w_cur, final_g=final_norm[None, :] if last else None, nxt=nxt)
        if not last and not side_cast:
            w_next = tuple(bf(w[nl]) for w in ffn_w[nw])
        return h, w_next

    for l in range(depth):
        h, w_cur = ffn(h, l, 0, w_cur)
        p, qkv, dt_t = _inproj(h, l, mix_g, w_proj, slab_width, heads, q_scale)
        ssd_o = _ssd(p, dt_t, off, l, ssd_prm, bsz, s_len)
        attn_o = _attn(qkv, bsz, s_len)
        h = _outproj(h, p, pool_width, ssd_o, attn_o, l, w_pool, pool_sc, w_o, s_len)
        h, w_cur = ffn(h, l, 1, w_cur)
    return h.reshape(bsz, s_len, d)
```

```python
import functools

import jax
import jax.numpy as jnp
from jax import lax
from jax.experimental import pallas as pl
from jax.experimental.pallas import tpu as pltpu

F32 = jnp.float32
BF16 = jnp.bfloat16

POOL_WINDOWS = (2, 4, 8, 16)
SSD_HEAD_DIM = 64
SSD_GROUPS = 2
SSD_STATE = 128
SSD_CONV = 4
ATTN_HEAD_DIM = 64
RMS_EPS = 1e-6
LOG2E = 1.4426950408889634
EXP2_CLAMP = 126.0
SKIP_BITS = 160.0

LANES = 128
SUBLANES = 8
VMEM_LIMIT = 60 * 1024 * 1024

FFN_TM = 1024
FFN_TF = 512
PROJ_TM = 512
OUT_TM = 512
POOL_HALO = 16
SSD_CHUNK = 256
CONV_HALO = 8
ATTN_TQ = 512
ATTN_TILES = 4
ATTN_SUB = 128
ATTN_WALK = 256


def _rms(x, g):
    ms = jnp.mean(x * x, axis=-1, keepdims=True)
    return x * lax.rsqrt(ms + RMS_EPS) * g


def _silu(x):
    h = 0.5 * x
    return h + h * jnp.tanh(h)


def _softplus(x):
    return jnp.maximum(x, 0.0) + jnp.log1p(jnp.exp(-jnp.abs(x)))


def _split3(x):
    x1 = x.astype(BF16)
    r1 = x - x1.astype(F32)
    x2 = r1.astype(BF16)
    x3 = (r1 - x2.astype(F32)).astype(BF16)
    return x1, x2, x3


def _const_spec(shape):
    nd = len(shape)
    return pl.BlockSpec(shape, lambda *_: (0,) * nd, pipeline_mode=pl.Buffered(1))


def _layer_spec(arr, l):
    tail = arr.shape[1:]
    return pl.BlockSpec((None,) + tail, lambda *_: (l,) + (0,) * len(tail), pipeline_mode=pl.Buffered(1))


def _params(sem):
    return pltpu.CompilerParams(dimension_semantics=sem, vmem_limit_bytes=VMEM_LIMIT)


def _ffn_kernel(h_ref, g_ref, wg_ref, wu_ref, wd_ref, *rest, final_norm, cast_next):
    rest = list(rest)
    fg_ref = rest.pop(0) if final_norm else None
    nxt = [rest.pop(0) for _ in range(3)] if cast_next else []
    o_ref = rest.pop(0)
    cast = [rest.pop(0) for _ in range(3)] if cast_next else []
    (u_ref,) = rest
    j = pl.program_id(1)

    @pl.when(j == 0)
    def _():
        h = h_ref[...]
        u_ref[...] = _rms(h, g_ref[...]).astype(BF16)
        o_ref[...] = h

    u = u_ref[...]
    a = jnp.dot(u, wg_ref[...], preferred_element_type=F32)
    b = jnp.dot(u, wu_ref[...], preferred_element_type=F32)
    act = (0.5 * _silu(a)) * b
    o_ref[...] += jnp.dot(act.astype(BF16), wd_ref[...], preferred_element_type=F32)

    for src_ref, dst_ref in zip(nxt, cast):
        dst_ref[...] = src_ref[...].astype(BF16)

    if final_norm:
        @pl.when(j == pl.num_programs(1) - 1)
        def _():
            o_ref[...] = _rms(o_ref[...], fg_ref[...])


def _ffn_can_cast(t, d):
    ni = t // min(FFN_TM, t)
    return d % ni == 0 and (d // ni) % LANES == 0


def _ffn(h, l, g, wg, wu, wd, final_g=None, nxt=None):
    t, d = h.shape
    f = wg.shape[1]
    tm, tf = min(FFN_TM, t), FFN_TF
    ni = t // tm
    assert t % tm == 0 and f % tf == 0
    in_specs = [
        pl.BlockSpec((tm, d), lambda i, j: (i, 0)),
        _layer_spec(g, l),
        pl.BlockSpec((d, tf), lambda i, j: (0, j)),
        pl.BlockSpec((d, tf), lambda i, j: (0, j)),
        pl.BlockSpec((tf, d), lambda i, j: (j, 0)),
    ]
    args = [h, g, wg, wu, wd]
    out_specs = [pl.BlockSpec((tm, d), lambda i, j: (i, 0))]
    out_shape = [jax.ShapeDtypeStruct((t, d), F32)]
    if final_g is not None:
        in_specs.append(_const_spec(final_g.shape))
        args.append(final_g)
    if nxt is not None:
        assert _ffn_can_cast(t, d)
        ln, rb = nxt[0], d // ni
        in_specs += [pl.BlockSpec((None, rb, tf), lambda i, j: (ln, i, j)),
                     pl.BlockSpec((None, rb, tf), lambda i, j: (ln, i, j)),
                     pl.BlockSpec((None, tf, rb), lambda i, j: (ln, j, i))]
        args += list(nxt[1:])
        out_specs += [pl.BlockSpec((rb, tf), lambda i, j: (i, j)),
                      pl.BlockSpec((rb, tf), lambda i, j: (i, j)),
                      pl.BlockSpec((tf, rb), lambda i, j: (j, i))]
        out_shape += [jax.ShapeDtypeStruct((d, f), BF16), jax.ShapeDtypeStruct((d, f), BF16),
                      jax.ShapeDtypeStruct((f, d), BF16)]
    outs = pl.pallas_call(
        functools.partial(_ffn_kernel, final_norm=final_g is not None, cast_next=nxt is not None),
        grid=(ni, f // tf),
        in_specs=in_specs,
        out_specs=out_specs,
        out_shape=out_shape,
        scratch_shapes=[pltpu.VMEM((tm, d), BF16)],
        compiler_params=_params(("parallel", "arbitrary")),
        name="ffn_final" if final_g is not None else "ffn",
    )(*args)
    return outs[0], tuple(outs[1:])


def _inproj_kernel(h_ref, g_ref, w_ref, p_ref, qkv_ref, dtt_ref, wq_ref, *, q_width, q_scale):
    npw, nq = p_ref.shape[1], qkv_ref.shape[1]
    heads, n_in = dtt_ref.shape[0], w_ref.shape[1]
    o_dt = npw - LANES

    @pl.when(pl.program_id(0) == 0)
    def _():
        r = lax.broadcasted_iota(jnp.int32, (LANES, LANES), 0)
        c = lax.broadcasted_iota(jnp.int32, (LANES, LANES), 1)
        head = jnp.where(r == c + heads, 1.0, 0.0).astype(BF16)
        tail = jnp.where(r + (LANES - heads) == c, 1.0, 0.0).astype(BF16)
        for t in range(nq // LANES):
            lo = o_dt + (t + 1) * LANES
            nxt = w_ref[:, lo:min(lo + LANES, n_in)]
            tile = (jnp.dot(w_ref[:, lo - LANES:lo], head, preferred_element_type=F32)
                    + jnp.dot(nxt, tail[:nxt.shape[1], :], preferred_element_type=F32))
            wq_ref[:, t * LANES:(t + 1) * LANES] = tile.astype(BF16)

    u = _rms(h_ref[...], g_ref[...]).astype(BF16)
    p_ref[...] = jnp.dot(u, w_ref[:, :npw], preferred_element_type=F32)
    col_scale = jnp.where(lax.broadcasted_iota(jnp.int32, (1, nq), 1) < q_width, q_scale, 1.0)
    qkv_ref[...] = (jnp.dot(u, wq_ref[...], preferred_element_type=F32) * col_scale).astype(BF16)
    dt_all = lax.dot_general(w_ref[:, o_dt:npw], u, (((0,), (1,)), ((), ())), preferred_element_type=F32)
    dtt_ref[...] = dt_all[:heads, :]


def _inproj(h, l, g, w, npw, heads, q_scale):
    t, d = h.shape
    nq = w.shape[2] - (npw - LANES) - heads
    tm = min(PROJ_TM, t)
    assert t % tm == 0 and npw % LANES == 0 and nq % (3 * LANES) == 0 and npw <= w.shape[2]
    return pl.pallas_call(
        functools.partial(_inproj_kernel, q_width=nq // 3, q_scale=q_scale),
        grid=(t // tm,),
        in_specs=[
            pl.BlockSpec((tm, d), lambda i: (i, 0)),
            _layer_spec(g, l),
            _layer_spec(w, l),
        ],
        out_specs=[
            pl.BlockSpec((tm, npw), lambda i: (i, 0)),
            pl.BlockSpec((tm, nq), lambda i: (i, 0)),
            pl.BlockSpec((heads, tm), lambda i: (0, i)),
        ],
        out_shape=[
            jax.ShapeDtypeStruct((t, npw), F32),
            jax.ShapeDtypeStruct((t, nq), BF16),
            jax.ShapeDtypeStruct((heads, t), F32),
        ],
        scratch_shapes=[pltpu.VMEM((d, nq), BF16)],
        compiler_params=_params(("arbitrary",)),
        name="inproj",
    )(h, g, w)


def _pool_reset(buf_ref, seq_tile):
    @pl.when(seq_tile == 0)
    def _():
        buf_ref[0:POOL_HALO, :] = jnp.zeros((POOL_HALO, buf_ref.shape[1]), F32)


def _pool_tile(x_ref, buf_ref, w_ref, sc_ref, seq_tile):
    ts, width = x_ref.shape
    group = width // len(POOL_WINDOWS)
    halo = POOL_HALO
    buf_ref[halo:halo + ts, :] = x_ref[...]
    pos = (seq_tile * ts + 1 + lax.broadcasted_iota(jnp.int32, (ts, 1), 0)).astype(F32)
    outs = []
    for i, w in enumerate(POOL_WINDOWS):
        cols = slice(i * group, (i + 1) * group)
        v = buf_ref[halo:halo + ts, cols]
        acc = v
        for k in range(1, w):
            acc = acc + buf_ref[halo - k:halo - k + ts, cols]
        mean = acc / jnp.minimum(pos, float(w))
        outs.append(jnp.dot((mean - v).astype(BF16), w_ref[i], preferred_element_type=F32))
    buf_ref[0:halo, :] = buf_ref[ts:ts + halo, :]
    return jnp.concatenate(outs, axis=-1) * sc_ref[...]


SSD_PARAMS = ("conv_w", "conv_b", "dt_bias", "dt_bias_t", "a_log", "a_log_t", "d_skip", "norm_g")


def _ssd_kernel(*refs, chunk, heads):
    L, P, N, G = chunk, SSD_HEAD_DIM, SSD_STATE, SSD_GROUPS
    z_refs, xs_refs = refs[:G], refs[G:2 * G]
    (bc_ref, dt_ref, dtt_ref, cw_ref, cb_ref, dtb_ref, dtbt_ref, alog_ref, alogt_ref,
     dskip_ref, ng_ref, o_ref, xbuf_ref, bbuf_ref, state_ref) = refs[2 * G:]
    hpg = heads // G
    gw = hpg * P
    halo = CONV_HALO

    @pl.when(pl.program_id(1) == 0)
    def _():
        xbuf_ref[0:halo, :] = jnp.zeros((halo, xbuf_ref.shape[1]), F32)
        bbuf_ref[0:halo, :] = jnp.zeros((halo, bbuf_ref.shape[1]), F32)
        state_ref[...] = jnp.zeros(state_ref.shape, F32)

    for g in range(G):
        xbuf_ref[halo:halo + L, g * gw:(g + 1) * gw] = xs_refs[g][...]
    bbuf_ref[halo:halo + L, :] = bc_ref[...]

    def conv_silu(buf_ref, c0, c1):
        x = buf_ref[...]
        tiles = x.reshape((halo + L) // SUBLANES, SUBLANES, x.shape[1])
        sub = lax.broadcasted_iota(jnp.int32, (1, SUBLANES, 1), 1)
        t0 = halo // SUBLANES
        y = x[halo:, :] * cw_ref[SSD_CONV - 1:SSD_CONV, c0:c1]
        for s in range(1, SSD_CONV):
            r = pltpu.roll(tiles, s, axis=1)
            shifted = jnp.where(sub < s, r[t0 - 1:-1], r[t0:]).reshape(L, x.shape[1])
            y = y + shifted * cw_ref[SSD_CONV - 1 - s:SSD_CONV - s, c0:c1]
        return _silu(y + cb_ref[:, c0:c1])

    xs = conv_silu(xbuf_ref, 0, heads * P)
    bc = conv_silu(bbuf_ref, heads * P, cw_ref.shape[1])
    xbuf_ref[0:halo, :] = xbuf_ref[L:L + halo, :]
    bbuf_ref[0:halo, :] = bbuf_ref[L:L + halo, :]

    head_lane = lax.broadcasted_iota(jnp.int32, (1, LANES), 1) < heads
    dt = jnp.where(head_lane, _softplus(dt_ref[...] + dtb_ref[...]), 0.0)
    a = (-LOG2E * jnp.exp(alog_ref[...])) * dt
    a_t = (-LOG2E * jnp.exp(alogt_ref[...])) * _softplus(dtt_ref[...] + dtbt_ref[...])

    row = lax.broadcasted_iota(jnp.int32, (L, L), 0)
    col = lax.broadcasted_iota(jnp.int32, (L, L), 1)
    causal = row >= col
    tri = jnp.where(causal, 1.0, 0.0).astype(BF16)
    tri_t = jnp.where(row <= col, 1.0, 0.0).astype(BF16)
    a3, a2, a1 = reversed(_split3(a))
    a_cs = (jnp.dot(tri, a3, preferred_element_type=F32) + jnp.dot(tri, a2, preferred_element_type=F32)
            + jnp.dot(tri, a1, preferred_element_type=F32))
    t3, t2, t1 = reversed(_split3(a_t))
    at_cs = (jnp.dot(t3, tri_t, preferred_element_type=F32) + jnp.dot(t2, tri_t, preferred_element_type=F32)
             + jnp.dot(t1, tri_t, preferred_element_type=F32))

    e_row = lax.broadcasted_iota(jnp.int32, (2 * LANES, heads * P), 0) & (LANES - 1)
    e_col = lax.broadcasted_iota(jnp.int32, (2 * LANES, heads * P), 1) // P
    e2 = jnp.where(e_row == e_col, 1.0, 0.0).astype(BF16)

    def expand(v):
        hi = v.astype(BF16)
        lo = (v - hi.astype(F32)).astype(BF16)
        return jnp.dot(jnp.concatenate([hi, lo], axis=1), e2, preferred_element_type=F32)

    a_last = a_cs[L - 1:L, :]
    dt_e = expand(dt)
    ea_e = expand(jnp.exp2(a_cs))
    wd_e = expand(jnp.exp2(a_last - a_cs))
    xdt = xs * dt_e
    xdt_b = xdt.astype(BF16)
    xw_b = (xdt * wd_e).astype(BF16)
    first = lax.broadcasted_iota(jnp.int32, (L, 2 * P), 1) < P
    zero = jnp.zeros((L, 2 * P), BF16)

    for g in range(G):
        sl = slice(g * gw, (g + 1) * gw)
        b_g = bc[:, g * N:(g + 1) * N].astype(BF16)
        c_g = bc[:, (G + g) * N:(G + g + 1) * N].astype(BF16)
        cb = lax.dot_general(c_g, b_g, (((1,), (1,)), ((), ())), preferred_element_type=F32)
        st = state_ref[g]
        y_off = jnp.dot(c_g, st.astype(BF16), preferred_element_type=F32) * ea_e[:, sl]
        upd = lax.dot_general(b_g, xw_b[:, sl], (((0,), (0,)), ((), ())), preferred_element_type=F32)
        state_ref[g] = ea_e[L - 1:L, sl] * st + upd
        ys = []
        for e in range(0, hpg, 2):
            h0 = g * hpg + e
            ms = []
            for h in (h0, h0 + 1):
                acol = a_cs[:, h:h + 1]
                arow = at_cs[h:h + 1, :]
                decay = jnp.where(causal, jnp.exp2(acol - arow), 0.0)
                ms.append((cb * decay).astype(BF16))
            xp = xdt_b[:, h0 * P:(h0 + 2) * P]
            rhs = jnp.concatenate([jnp.where(first, xp, zero), jnp.where(first, zero, xp)], axis=0)
            ys.append(jnp.dot(jnp.concatenate(ms, axis=1), rhs, preferred_element_type=F32))
        y = jnp.concatenate(ys, axis=1) + y_off + dskip_ref[:, sl] * xs[:, sl]
        y = y * _silu(z_refs[g][...])
        y = y * lax.rsqrt(jnp.mean(y * y, axis=-1, keepdims=True) + RMS_EPS)
        o_ref[:, sl] = (y * ng_ref[:, sl]).astype(BF16)


def _ssd(p, dt_t, off, l, prm, bsz, s_len):
    heads = dt_t.shape[0]
    width = heads * SSD_HEAD_DIM
    G = SSD_GROUPS
    gw = width // G
    bcw = 2 * G * SSD_STATE
    L = min(SSD_CHUNK, s_len)
    nc = s_len // L
    assert s_len % L == 0 and L >= CONV_HALO and heads % (2 * G) == 0
    assert off["z"] % gw == 0 and off["xs"] % gw == 0 and off["bc"] % bcw == 0 and off["dt"] % LANES == 0

    def tok(w, col):
        return pl.BlockSpec((L, w), lambda b, c: (b * nc + c, col // w))

    in_specs = (
        [tok(gw, off["z"] + g * gw) for g in range(G)] + [tok(gw, off["xs"] + g * gw) for g in range(G)]
        + [tok(bcw, off["bc"]), tok(LANES, off["dt"]),
           pl.BlockSpec((heads, L), lambda b, c: (0, b * nc + c))]
        + [_layer_spec(prm[k], l) for k in SSD_PARAMS])
    return pl.pallas_call(
        functools.partial(_ssd_kernel, chunk=L, heads=heads),
        grid=(bsz, nc),
        in_specs=in_specs,
        out_specs=pl.BlockSpec((L, width), lambda b, c: (b * nc + c, 0)),
        out_shape=jax.ShapeDtypeStruct((bsz * s_len, width), BF16),
        scratch_shapes=[
            pltpu.VMEM((CONV_HALO + L, width), F32),
            pltpu.VMEM((CONV_HALO + L, bcw), F32),
            pltpu.VMEM((G, SSD_STATE, gw), F32),
        ],
        compiler_params=_params(("arbitrary", "arbitrary")),
        name="ssd",
    )(*([p] * (2 * G + 2)), dt_t, *[prm[k] for k in SSD_PARAMS])


def _attn_kernel(q_ref, k_ref, v_ref, o_ref, kbd_ref, vbd_ref, acc_ref, rn_ref, *, tq, tiles, nsub, walk):
    sub, hd = ATTN_SUB, ATTN_HEAD_DIM
    i = pl.program_id(2)
    nblk = k_ref.shape[0] // sub

    @pl.when(i == 0)
    def _():
        first = lax.broadcasted_iota(jnp.int32, (sub, 2 * hd), 1) < hd
        zero = jnp.zeros((sub, 2 * hd), BF16)

        def build(n, carry):
            kb = k_ref[pl.ds(n * sub, sub), :]
            vb = v_ref[pl.ds(n * sub, sub), :]
            kbd_ref[n] = jnp.concatenate([jnp.where(first, kb, zero), jnp.where(first, zero, kb)], axis=0)
            vbd_ref[n] = jnp.concatenate([jnp.where(first, vb, zero), jnp.where(first, zero, vb)], axis=0)
            return carry

        lax.fori_loop(0, nblk, build, 0)

    q = q_ref[...]
    rj = lax.broadcasted_iota(jnp.int32, (2 * sub, 2 * sub), 0) & (sub - 1)
    cj = lax.broadcasted_iota(jnp.int32, (2 * sub, 2 * sub), 1)
    tri = (cj < sub) & (rj >= cj)
    uu0 = jnp.where(tri | ((cj >= sub) & (cj < sub + hd)), 1.0, 0.0).astype(BF16)
    uu1 = jnp.where(tri | (cj >= sub + hd), 1.0, 0.0).astype(BF16)
    acc_ref[...] = jnp.zeros(acc_ref.shape, F32)
    rn_ref[...] = jnp.zeros(rn_ref.shape, F32)

    diag_valid = ((lax.broadcasted_iota(jnp.int32, (sub, 2 * sub), 1) & (sub - 1))
                  < lax.broadcasted_iota(jnp.int32, (sub, 2 * sub), 0))

    def mask_top(x, on_diag):
        if not on_diag:
            return x
        top = jnp.where(diag_valid, x[:sub], 0.0)
        return top if x.shape[0] == sub else jnp.concatenate([top, x[sub:]], axis=0)

    def sub_block(n, r0, r1, on_diag):
        z = lax.dot_general(q[r0:r1], kbd_ref[n], (((1,), (1,)), ((), ())), preferred_element_type=F32)
        sp = jnp.maximum(z, jnp.log(1.0 + jnp.exp2(jnp.minimum(z, EXP2_CLAMP))) * LOG2E)
        sp = mask_top(sp, on_diag)
        hi = sp.astype(BF16)
        lo = (sp - hi.astype(F32)).astype(BF16)
        c0 = jnp.dot(jnp.concatenate([hi[:, :sub], lo[:, :sub]], axis=1), uu0, preferred_element_type=F32)
        c1 = jnp.dot(jnp.concatenate([hi[:, sub:], lo[:, sub:]], axis=1), uu1, preferred_element_type=F32)
        w = mask_top(jnp.exp2(z - jnp.concatenate([c0[:, :sub], c1[:, :sub]], axis=1)), on_diag)
        pv = jnp.dot(w.astype(BF16), vbd_ref[n], preferred_element_type=F32)
        rn = rn_ref[r0:r1, :]
        acc_ref[r0:r1, :] = acc_ref[r0:r1, :] + pv * jnp.exp2(rn)
        rn_ref[r0:r1, :] = rn - (c0[:, sub:] + c1[:, sub:])

    for hf in range(tiles):
        for sb in reversed(range(nsub)):
            sub_block((i * tiles + hf) * nsub + sb, hf * tq + sb * sub, (hf + 1) * tq, True)

    for hf in range(tiles):
        r0, r1 = hf * tq, (hf + 1) * tq
        first = (i * tiles + hf) * nsub

        def live(r0=r0, r1=r1):
            return (jnp.max(rn_ref[r0:r1, :]) > -SKIP_BITS).astype(jnp.int32)

        def more(carry, first=first):
            t, alive = carry
            return (t < first // walk) & (alive > 0)

        def body(carry, r0=r0, r1=r1, first=first, live=live):
            t, _ = carry
            for k in range(walk):
                sub_block(first - 1 - t * walk - k, r0, r1, False)
            return t + 1, live()

        lax.while_loop(more, body, (jnp.int32(0), live()))
    o_ref[...] = acc_ref[...].astype(BF16)


def _attn(qkv, bsz, s_len):
    width = qkv.shape[1] // 3
    pair = 2 * ATTN_HEAD_DIM
    npair = width // pair
    tq = min(ATTN_TQ, s_len)
    tiles = ATTN_TILES if s_len % (ATTN_TILES * tq) == 0 else 1
    tb = tiles * tq
    nq = s_len // tb
    nsub = tq // ATTN_SUB
    walk = min(ATTN_WALK, tq) // ATTN_SUB
    assert s_len % tb == 0 and tq % ATTN_SUB == 0 and width % pair == 0 and pair == LANES and nsub % walk == 0
    nblk = s_len // ATTN_SUB
    return pl.pallas_call(
        functools.partial(_attn_kernel, tq=tq, tiles=tiles, nsub=nsub, walk=walk),
        grid=(bsz, npair, nq),
        in_specs=[
            pl.BlockSpec((tb, pair), lambda b, p, i: (b * nq + i, p)),
            pl.BlockSpec((s_len, pair), lambda b, p, i: (b, npair + p)),
            pl.BlockSpec((s_len, pair), lambda b, p, i: (b, 2 * npair + p)),
        ],
        out_specs=pl.BlockSpec((tb, pair), lambda b, p, i: (b * nq + i, p)),
        out_shape=jax.ShapeDtypeStruct((bsz * s_len, width), BF16),
        scratch_shapes=[
            pltpu.VMEM((nblk, 2 * ATTN_SUB, pair), BF16),
            pltpu.VMEM((nblk, 2 * ATTN_SUB, pair), BF16),
            pltpu.VMEM((tb, pair), F32),
            pltpu.VMEM((tb, pair), F32),
        ],
        compiler_params=_params(("arbitrary", "arbitrary", "arbitrary")),
        name="attn",
    )(qkv, qkv, qkv)


def _outproj_kernel(h_ref, x_ref, s_ref, a_ref, pw_ref, psc_ref, w_ref, o_ref, buf_ref, *, tiles_per_seq):
    seq_tile = pl.program_id(0) % tiles_per_seq
    _pool_reset(buf_ref, seq_tile)
    r0 = x_ref.shape[1]
    r1 = r0 + s_ref.shape[1]
    acc = jnp.dot(s_ref[...], w_ref[r0:r1, :], preferred_element_type=F32)
    acc = acc + jnp.dot(a_ref[...], w_ref[r1:, :], preferred_element_type=F32)
    pool_o = _pool_tile(x_ref, buf_ref, pw_ref, psc_ref, seq_tile)
    acc = acc + jnp.dot(pool_o.astype(BF16), w_ref[:r0, :], preferred_element_type=F32)
    o_ref[...] = h_ref[...] + acc


def _outproj(h, p, pool_width, ssd_o, attn_o, l, w_pool, pool_scale, w, s_len):
    t, d = h.shape
    tm = min(OUT_TM, s_len)
    assert s_len % tm == 0 and tm >= POOL_HALO and pool_width + ssd_o.shape[1] + attn_o.shape[1] == w.shape[1]

    def tok(width):
        return pl.BlockSpec((tm, width), lambda i: (i, 0))

    return pl.pallas_call(
        functools.partial(_outproj_kernel, tiles_per_seq=s_len // tm),
        grid=(t // tm,),
        in_specs=[tok(d), tok(pool_width), tok(ssd_o.shape[1]), tok(attn_o.shape[1]),
                  _layer_spec(w_pool, l), _layer_spec(pool_scale, l), _layer_spec(w, l)],
        out_specs=tok(d),
        out_shape=jax.ShapeDtypeStruct((t, d), F32),
        scratch_shapes=[pltpu.VMEM((POOL_HALO + tm, pool_width), F32)],
        compiler_params=_params(("arbitrary",)),
        name="outproj",
    )(h, p, ssd_o, attn_o, w_pool, pool_scale, w)


def _rows(v):
    return v[:, None, :]


def _pad_lanes(v):
    return _rows(jnp.pad(v, ((0, 0), (0, LANES - v.shape[1]))))


def kernel(x, ffn1_norm, ffn1_w_gate, ffn1_w_up, ffn1_w_down, mix_norm, w_in, pool_w, pool_scale, conv_w, conv_b, dt_bias, a_log, d_skip, ssd_norm, w_out, ffn2_norm, ffn2_w_gate, ffn2_w_up, ffn2_w_down, final_norm):
    bsz, s_len, d = x.shape
    depth = w_in.shape[0]
    pool_width = pool_scale.shape[1]
    ssd_width = ssd_norm.shape[1]
    heads = dt_bias.shape[1]
    xbc_width = conv_w.shape[2]
    bcw = xbc_width - ssd_width
    attn_width = (w_in.shape[2] - pool_width - ssd_width - xbc_width - heads) // 3
    assert heads <= LANES and heads * SSD_HEAD_DIM == ssd_width and bcw == 2 * SSD_GROUPS * SSD_STATE
    o_z = pool_width
    o_xs = o_z + ssd_width
    o_bc = o_xs + ssd_width
    o_dt = o_bc + bcw
    o_q = o_dt + heads
    slab_width = o_dt + LANES
    assert o_dt % LANES == 0 and pool_width % LANES == 0 and o_q + 3 * attn_width == w_in.shape[2]
    off = {"z": o_z, "xs": o_xs, "bc": o_bc, "dt": o_dt}
    q_scale = ATTN_HEAD_DIM ** -0.5 * LOG2E

    bf = lambda w: w.astype(BF16)
    w_proj = bf(w_in)
    ffn_g = (_rows(ffn1_norm), _rows(ffn2_norm))
    ffn_w = ((ffn1_w_gate, ffn1_w_up, ffn1_w_down), (ffn2_w_gate, ffn2_w_up, ffn2_w_down))
    mix_g, w_pool, pool_sc, w_o = _rows(mix_norm), bf(pool_w), _rows(pool_scale), bf(w_out)
    ssd_prm = {
        "conv_w": conv_w, "conv_b": _rows(conv_b),
        "dt_bias": _pad_lanes(dt_bias), "dt_bias_t": dt_bias[:, :, None],
        "a_log": _pad_lanes(a_log), "a_log_t": a_log[:, :, None],
        "d_skip": _rows(jnp.repeat(d_skip, SSD_HEAD_DIM, axis=1)), "norm_g": _rows(ssd_norm),
    }

    h = x.reshape(bsz * s_len, d)
    side_cast = _ffn_can_cast(bsz * s_len, d)
    w_cur = tuple(bf(w[0]) for w in ffn_w[0])

    def ffn(h, l, which, w_cur):
        last = l == depth - 1 and which == 1
        nl, nw = (l, 1) if which == 0 else (l + 1, 0)
        nxt = None if last or not side_cast else (nl,) + ffn_w[nw]
        h, w_next = _ffn(h, l, ffn_g[which], *w_cur, final_g=final_norm[None, :] if last else None, nxt=nxt)
        if not last and not side_cast:
            w_next = tuple(bf(w[nl]) for w in ffn_w[nw])
        return h, w_next

    for l in range(depth):
        h, w_cur = ffn(h, l, 0, w_cur)
        p, qkv, dt_t = _inproj(h, l, mix_g, w_proj, slab_width, heads, q_scale)
        ssd_o = _ssd(p, dt_t, off, l, ssd_prm, bsz, s_len)
        attn_o = _attn(qkv, bsz, s_len)
        h = _outproj(h, p, pool_width, ssd_o, attn_o, l, w_pool, pool_sc, w_o, s_len)
        h, w_cur = ffn(h, l, 1, w_cur)
    return h.reshape(bsz, s_len, d)
```

```python
import functools

import jax
import jax.numpy as jnp
from jax import lax
from jax.experimental import pallas as pl
from jax.experimental.pallas import tpu as pltpu

F32 = jnp.float32
BF16 = jnp.bfloat16

POOL_WINDOWS = (2, 4, 8, 16)
SSD_HEAD_DIM = 64
SSD_GROUPS = 2
SSD_STATE = 128
SSD_CONV = 4
ATTN_HEAD_DIM = 64
RMS_EPS = 1e-6
LOG2E = 1.4426950408889634
EXP2_CLAMP = 126.0
SKIP_BITS = 160.0

LANES = 128
SUBLANES = 8
VMEM_LIMIT = 60 * 1024 * 1024

FFN_TM = 1024
FFN_TF = 512
PROJ_TM = 512
PROJ_STAGE_ROWS = 128
OUT_TM = 512
POOL_HALO = 16
SSD_CHUNK = 256
CONV_HALO = 8
ATTN_TQ = 512
ATTN_TILES = 4
ATTN_SUB = 128
ATTN_WALK = 256


def _rms(x, g):
    ms = jnp.mean(x * x, axis=-1, keepdims=True)
    return x * lax.rsqrt(ms + RMS_EPS) * g


def _silu(x):
    h = 0.5 * x
    return h + h * jnp.tanh(h)


def _softplus(x):
    return jnp.maximum(x, 0.0) + jnp.log1p(jnp.exp(-jnp.abs(x)))


def _split3(x):
    x1 = x.astype(BF16)
    r1 = x - x1.astype(F32)
    x2 = r1.astype(BF16)
    x3 = (r1 - x2.astype(F32)).astype(BF16)
    return x1, x2, x3


def _const_spec(shape):
    nd = len(shape)
    return pl.BlockSpec(shape, lambda *_: (0,) * nd, pipeline_mode=pl.Buffered(1))


def _layer_spec(arr, l):
    tail = arr.shape[1:]
    return pl.BlockSpec((None,) + tail, lambda *_: (l,) + (0,) * len(tail), pipeline_mode=pl.Buffered(1))


def _params(sem):
    return pltpu.CompilerParams(dimension_semantics=sem, vmem_limit_bytes=VMEM_LIMIT)


def _ffn_kernel(h_ref, g_ref, wg_ref, wu_ref, wd_ref, *rest, final_norm, cast_next):
    rest = list(rest)
    fg_ref = rest.pop(0) if final_norm else None
    nxt = [rest.pop(0) for _ in range(3)] if cast_next else []
    o_ref = rest.pop(0)
    cast = [rest.pop(0) for _ in range(3)] if cast_next else []
    (u_ref,) = rest
    j = pl.program_id(1)

    @pl.when(j == 0)
    def _():
        h = h_ref[...]
        u_ref[...] = _rms(h, g_ref[...]).astype(BF16)
        o_ref[...] = h

    u = u_ref[...]
    a = jnp.dot(u, wg_ref[...], preferred_element_type=F32)
    b = jnp.dot(u, wu_ref[...], preferred_element_type=F32)
    act = (0.5 * _silu(a)) * b
    o_ref[...] += jnp.dot(act.astype(BF16), wd_ref[...], preferred_element_type=F32)

    for src_ref, dst_ref in zip(nxt, cast):
        dst_ref[...] = src_ref[...].astype(BF16)

    if final_norm:
        @pl.when(j == pl.num_programs(1) - 1)
        def _():
            o_ref[...] = _rms(o_ref[...], fg_ref[...])


def _ffn_can_cast(t, d):
    ni = t // min(FFN_TM, t)
    return d % ni == 0 and (d // ni) % LANES == 0


def _ffn(h, l, g, wg, wu, wd, final_g=None, nxt=None):
    t, d = h.shape
    f = wg.shape[1]
    tm, tf = min(FFN_TM, t), FFN_TF
    ni = t // tm
    assert t % tm == 0 and f % tf == 0
    in_specs = [
        pl.BlockSpec((tm, d), lambda i, j: (i, 0)),
        _layer_spec(g, l),
        pl.BlockSpec((d, tf), lambda i, j: (0, j)),
        pl.BlockSpec((d, tf), lambda i, j: (0, j)),
        pl.BlockSpec((tf, d), lambda i, j: (j, 0)),
    ]
    args = [h, g, wg, wu, wd]
    out_specs = [pl.BlockSpec((tm, d), lambda i, j: (i, 0))]
    out_shape = [jax.ShapeDtypeStruct((t, d), F32)]
    if final_g is not None:
        in_specs.append(_const_spec(final_g.shape))
        args.append(final_g)
    if nxt is not None:
        assert _ffn_can_cast(t, d)
        ln, rb = nxt[0], d // ni
        in_specs += [pl.BlockSpec((None, rb, tf), lambda i, j: (ln, i, j)),
                     pl.BlockSpec((None, rb, tf), lambda i, j: (ln, i, j)),
                     pl.BlockSpec((None, tf, rb), lambda i, j: (ln, j, i))]
        args += list(nxt[1:])
        out_specs += [pl.BlockSpec((rb, tf), lambda i, j: (i, j)),
                      pl.BlockSpec((rb, tf), lambda i, j: (i, j)),
                      pl.BlockSpec((tf, rb), lambda i, j: (j, i))]
        out_shape += [jax.ShapeDtypeStruct((d, f), BF16), jax.ShapeDtypeStruct((d, f), BF16),
                      jax.ShapeDtypeStruct((f, d), BF16)]
    outs = pl.pallas_call(
        functools.partial(_ffn_kernel, final_norm=final_g is not None, cast_next=nxt is not None),
        grid=(ni, f // tf),
        in_specs=in_specs,
        out_specs=out_specs,
        out_shape=out_shape,
        scratch_shapes=[pltpu.VMEM((tm, d), BF16)],
        compiler_params=_params(("parallel", "arbitrary")),
        name="ffn_final" if final_g is not None else "ffn",
    )(*args)
    return outs[0], tuple(outs[1:])


def _inproj_kernel(h_ref, g_ref, w_hbm, p_ref, qkv_ref, dtt_ref, w_ref, wq_ref, stage_ref, sem, *,
                   layer, q_width, q_scale):
    npw, nq = p_ref.shape[1], qkv_ref.shape[1]
    heads, n_in = dtt_ref.shape[0], w_ref.shape[1]
    o_dt = npw - LANES
    rows = stage_ref.shape[1]
    nslab = w_ref.shape[0] // rows

    def slab_copy(c):
        return pltpu.make_async_copy(w_hbm.at[layer, pl.ds(c * rows, rows), :], stage_ref.at[c % 2], sem.at[c % 2])

    @pl.when(pl.program_id(0) == 0)
    def _():
        slab_copy(0).start()
        for c in range(nslab):
            if c + 1 < nslab:
                slab_copy(c + 1).start()
            slab_copy(c).wait()
            w_ref[c * rows:(c + 1) * rows, :] = stage_ref[c % 2].astype(BF16)

    @pl.when(pl.program_id(0) == 0)
    def _():
        r = lax.broadcasted_iota(jnp.int32, (LANES, LANES), 0)
        c = lax.broadcasted_iota(jnp.int32, (LANES, LANES), 1)
        head = jnp.where(r == c + heads, 1.0, 0.0).astype(BF16)
        tail = jnp.where(r + (LANES - heads) == c, 1.0, 0.0).astype(BF16)
        for t in range(nq // LANES):
            lo = o_dt + (t + 1) * LANES
            nxt = w_ref[:, lo:min(lo + LANES, n_in)]
            tile = (jnp.dot(w_ref[:, lo - LANES:lo], head, preferred_element_type=F32)
                    + jnp.dot(nxt, tail[:nxt.shape[1], :], preferred_element_type=F32))
            wq_ref[:, t * LANES:(t + 1) * LANES] = tile.astype(BF16)

    u = _rms(h_ref[...], g_ref[...]).astype(BF16)
    p_ref[...] = jnp.dot(u, w_ref[:, :npw], preferred_element_type=F32)
    col_scale = jnp.where(lax.broadcasted_iota(jnp.int32, (1, nq), 1) < q_width, q_scale, 1.0)
    qkv_ref[...] = (jnp.dot(u, wq_ref[...], preferred_element_type=F32) * col_scale).astype(BF16)
    dt_all = lax.dot_general(w_ref[:, o_dt:npw], u, (((0,), (1,)), ((), ())), preferred_element_type=F32)
    dtt_ref[...] = dt_all[:heads, :]


def _inproj(h, l, g, w, npw, heads, q_scale):
    t, d = h.shape
    n_in = w.shape[2]
    nq = n_in - (npw - LANES) - heads
    tm = min(PROJ_TM, t)
    rows = min(PROJ_STAGE_ROWS, d)
    assert t % tm == 0 and npw % LANES == 0 and nq % (3 * LANES) == 0 and npw <= n_in and d % rows == 0
    return pl.pallas_call(
        functools.partial(_inproj_kernel, layer=l, q_width=nq // 3, q_scale=q_scale),
        grid=(t // tm,),
        in_specs=[
            pl.BlockSpec((tm, d), lambda i: (i, 0)),
            _layer_spec(g, l),
            pl.BlockSpec(memory_space=pl.ANY),
        ],
        out_specs=[
            pl.BlockSpec((tm, npw), lambda i: (i, 0)),
            pl.BlockSpec((tm, nq), lambda i: (i, 0)),
            pl.BlockSpec((heads, tm), lambda i: (0, i)),
        ],
        out_shape=[
            jax.ShapeDtypeStruct((t, npw), F32),
            jax.ShapeDtypeStruct((t, nq), BF16),
            jax.ShapeDtypeStruct((heads, t), F32),
        ],
        scratch_shapes=[
            pltpu.VMEM((d, n_in), BF16),
            pltpu.VMEM((d, nq), BF16),
            pltpu.VMEM((2, rows, n_in), F32),
            pltpu.SemaphoreType.DMA((2,)),
        ],
        compiler_params=_params(("arbitrary",)),
        name="inproj",
    )(h, g, w)


def _pool_reset(buf_ref, seq_tile):
    @pl.when(seq_tile == 0)
    def _():
        buf_ref[0:POOL_HALO, :] = jnp.zeros((POOL_HALO, buf_ref.shape[1]), F32)


def _pool_tile(x_ref, buf_ref, w_ref, sc_ref, seq_tile):
    ts, width = x_ref.shape
    group = width // len(POOL_WINDOWS)
    halo = POOL_HALO
    buf_ref[halo:halo + ts, :] = x_ref[...]
    pos = (seq_tile * ts + 1 + lax.broadcasted_iota(jnp.int32, (ts, 1), 0)).astype(F32)
    outs = []
    for i, w in enumerate(POOL_WINDOWS):
        cols = slice(i * group, (i + 1) * group)
        v = buf_ref[halo:halo + ts, cols]
        acc = v
        for k in range(1, w):
            acc = acc + buf_ref[halo - k:halo - k + ts, cols]
        mean = acc / jnp.minimum(pos, float(w))
        outs.append(jnp.dot((mean - v).astype(BF16), w_ref[i], preferred_element_type=F32))
    buf_ref[0:halo, :] = buf_ref[ts:ts + halo, :]
    return jnp.concatenate(outs, axis=-1) * sc_ref[...]


SSD_PARAMS = ("conv_w", "conv_b", "dt_bias", "dt_bias_t", "a_log", "a_log_t", "d_skip", "norm_g")


def _ssd_kernel(*refs, chunk, heads):
    L, P, N, G = chunk, SSD_HEAD_DIM, SSD_STATE, SSD_GROUPS
    z_refs, xs_refs = refs[:G], refs[G:2 * G]
    (bc_ref, dt_ref, dtt_ref, cw_ref, cb_ref, dtb_ref, dtbt_ref, alog_ref, alogt_ref,
     dskip_ref, ng_ref, o_ref, xbuf_ref, bbuf_ref, state_ref) = refs[2 * G:]
    hpg = heads // G
    gw = hpg * P
    halo = CONV_HALO

    @pl.when(pl.program_id(1) == 0)
    def _():
        xbuf_ref[0:halo, :] = jnp.zeros((halo, xbuf_ref.shape[1]), F32)
        bbuf_ref[0:halo, :] = jnp.zeros((halo, bbuf_ref.shape[1]), F32)
        state_ref[...] = jnp.zeros(state_ref.shape, F32)

    for g in range(G):
        xbuf_ref[halo:halo + L, g * gw:(g + 1) * gw] = xs_refs[g][...]
    bbuf_ref[halo:halo + L, :] = bc_ref[...]

    def conv_silu(buf_ref, c0, c1):
        x = buf_ref[...]
        tiles = x.reshape((halo + L) // SUBLANES, SUBLANES, x.shape[1])
        sub = lax.broadcasted_iota(jnp.int32, (1, SUBLANES, 1), 1)
        t0 = halo // SUBLANES
        y = x[halo:, :] * cw_ref[SSD_CONV - 1:SSD_CONV, c0:c1]
        for s in range(1, SSD_CONV):
            r = pltpu.roll(tiles, s, axis=1)
            shifted = jnp.where(sub < s, r[t0 - 1:-1], r[t0:]).reshape(L, x.shape[1])
            y = y + shifted * cw_ref[SSD_CONV - 1 - s:SSD_CONV - s, c0:c1]
        return _silu(y + cb_ref[:, c0:c1])

    xs = conv_silu(xbuf_ref, 0, heads * P)
    bc = conv_silu(bbuf_ref, heads * P, cw_ref.shape[1])
    xbuf_ref[0:halo, :] = xbuf_ref[L:L + halo, :]
    bbuf_ref[0:halo, :] = bbuf_ref[L:L + halo, :]

    head_lane = lax.broadcasted_iota(jnp.int32, (1, LANES), 1) < heads
    dt = jnp.where(head_lane, _softplus(dt_ref[...] + dtb_ref[...]), 0.0)
    a = (-LOG2E * jnp.exp(alog_ref[...])) * dt
    a_t = (-LOG2E * jnp.exp(alogt_ref[...])) * _softplus(dtt_ref[...] + dtbt_ref[...])

    row = lax.broadcasted_iota(jnp.int32, (L, L), 0)
    col = lax.broadcasted_iota(jnp.int32, (L, L), 1)
    causal = row >= col
    tri = jnp.where(causal, 1.0, 0.0).astype(BF16)
    tri_t = jnp.where(row <= col, 1.0, 0.0).astype(BF16)
    a3, a2, a1 = reversed(_split3(a))
    a_cs = (jnp.dot(tri, a3, preferred_element_type=F32) + jnp.dot(tri, a2, preferred_element_type=F32)
            + jnp.dot(tri, a1, preferred_element_type=F32))
    t3, t2, t1 = reversed(_split3(a_t))
    at_cs = (jnp.dot(t3, tri_t, preferred_element_type=F32) + jnp.dot(t2, tri_t, preferred_element_type=F32)
             + jnp.dot(t1, tri_t, preferred_element_type=F32))

    e_row = lax.broadcasted_iota(jnp.int32, (2 * LANES, heads * P), 0) & (LANES - 1)
    e_col = lax.broadcasted_iota(jnp.int32, (2 * LANES, heads * P), 1) // P
    e2 = jnp.where(e_row == e_col, 1.0, 0.0).astype(BF16)

    def expand(v):
        hi = v.astype(BF16)
        lo = (v - hi.astype(F32)).astype(BF16)
        return jnp.dot(jnp.concatenate([hi, lo], axis=1), e2, preferred_element_type=F32)

    a_last = a_cs[L - 1:L, :]
    dt_e = expand(dt)
    ea_e = expand(jnp.exp2(a_cs))
    wd_e = expand(jnp.exp2(a_last - a_cs))
    xdt = xs * dt_e
    xdt_b = xdt.astype(BF16)
    xw_b = (xdt * wd_e).astype(BF16)
    first = lax.broadcasted_iota(jnp.int32, (L, 2 * P), 1) < P
    zero = jnp.zeros((L, 2 * P), BF16)

    for g in range(G):
        sl = slice(g * gw, (g + 1) * gw)
        b_g = bc[:, g * N:(g + 1) * N].astype(BF16)
        c_g = bc[:, (G + g) * N:(G + g + 1) * N].astype(BF16)
        cb = lax.dot_general(c_g, b_g, (((1,), (1,)), ((), ())), preferred_element_type=F32)
        st = state_ref[g]
        y_off = jnp.dot(c_g, st.astype(BF16), preferred_element_type=F32) * ea_e[:, sl]
        upd = lax.dot_general(b_g, xw_b[:, sl], (((0,), (0,)), ((), ())), preferred_element_type=F32)
        state_ref[g] = ea_e[L - 1:L, sl] * st + upd
        ys = []
        for e in range(0, hpg, 2):
            h0 = g * hpg + e
            ms = []
            for h in (h0, h0 + 1):
                acol = a_cs[:, h:h + 1]
                arow = at_cs[h:h + 1, :]
                decay = jnp.where(causal, jnp.exp2(acol - arow), 0.0)
                ms.append((cb * decay).astype(BF16))
            xp = xdt_b[:, h0 * P:(h0 + 2) * P]
            rhs = jnp.concatenate([jnp.where(first, xp, zero), jnp.where(first, zero, xp)], axis=0)
            ys.append(jnp.dot(jnp.concatenate(ms, axis=1), rhs, preferred_element_type=F32))
        y = jnp.concatenate(ys, axis=1) + y_off + dskip_ref[:, sl] * xs[:, sl]
        y = y * _silu(z_refs[g][...])
        y = y * lax.rsqrt(jnp.mean(y * y, axis=-1, keepdims=True) + RMS_EPS)
        o_ref[:, sl] = (y * ng_ref[:, sl]).astype(BF16)


def _ssd(p, dt_t, off, l, prm, bsz, s_len):
    heads = dt_t.shape[0]
    width = heads * SSD_HEAD_DIM
    G = SSD_GROUPS
    gw = width // G
    bcw = 2 * G * SSD_STATE
    L = min(SSD_CHUNK, s_len)
    nc = s_len // L
    assert s_len % L == 0 and L >= CONV_HALO and heads % (2 * G) == 0
    assert off["z"] % gw == 0 and off["xs"] % gw == 0 and off["bc"] % bcw == 0 and off["dt"] % LANES == 0

    def tok(w, col):
        return pl.BlockSpec((L, w), lambda b, c: (b * nc + c, col // w))

    in_specs = (
        [tok(gw, off["z"] + g * gw) for g in range(G)] + [tok(gw, off["xs"] + g * gw) for g in range(G)]
        + [tok(bcw, off["bc"]), tok(LANES, off["dt"]),
           pl.BlockSpec((heads, L), lambda b, c: (0, b * nc + c))]
        + [_layer_spec(prm[k], l) for k in SSD_PARAMS])
    return pl.pallas_call(
        functools.partial(_ssd_kernel, chunk=L, heads=heads),
        grid=(bsz, nc),
        in_specs=in_specs,
        out_specs=pl.BlockSpec((L, width), lambda b, c: (b * nc + c, 0)),
        out_shape=jax.ShapeDtypeStruct((bsz * s_len, width), BF16),
        scratch_shapes=[
            pltpu.VMEM((CONV_HALO + L, width), F32),
            pltpu.VMEM((CONV_HALO + L, bcw), F32),
            pltpu.VMEM((G, SSD_STATE, gw), F32),
        ],
        compiler_params=_params(("arbitrary", "arbitrary")),
        name="ssd",
    )(*([p] * (2 * G + 2)), dt_t, *[prm[k] for k in SSD_PARAMS])


def _attn_kernel(q_ref, k_ref, v_ref, o_ref, kbd_ref, vbd_ref, acc_ref, rn_ref, *, tq, tiles, nsub, walk):
    sub, hd = ATTN_SUB, ATTN_HEAD_DIM
    i = pl.program_id(2)
    nblk = k_ref.shape[0] // sub

    @pl.when(i == 0)
    def _():
        first = lax.broadcasted_iota(jnp.int32, (sub, 2 * hd), 1) < hd
        zero = jnp.zeros((sub, 2 * hd), BF16)

        def build(n, carry):
            kb = k_ref[pl.ds(n * sub, sub), :]
            vb = v_ref[pl.ds(n * sub, sub), :]
            kbd_ref[n] = jnp.concatenate([jnp.where(first, kb, zero), jnp.where(first, zero, kb)], axis=0)
            vbd_ref[n] = jnp.concatenate([jnp.where(first, vb, zero), jnp.where(first, zero, vb)], axis=0)
            return carry

        lax.fori_loop(0, nblk, build, 0)

    q = q_ref[...]
    rj = lax.broadcasted_iota(jnp.int32, (2 * sub, 2 * sub), 0) & (sub - 1)
    cj = lax.broadcasted_iota(jnp.int32, (2 * sub, 2 * sub), 1)
    tri = (cj < sub) & (rj >= cj)
    uu0 = jnp.where(tri | ((cj >= sub) & (cj < sub + hd)), 1.0, 0.0).astype(BF16)
    uu1 = jnp.where(tri | (cj >= sub + hd), 1.0, 0.0).astype(BF16)
    acc_ref[...] = jnp.zeros(acc_ref.shape, F32)
    rn_ref[...] = jnp.zeros(rn_ref.shape, F32)

    diag_valid = ((lax.broadcasted_iota(jnp.int32, (sub, 2 * sub), 1) & (sub - 1))
                  < lax.broadcasted_iota(jnp.int32, (sub, 2 * sub), 0))

    def mask_top(x, on_diag):
        if not on_diag:
            return x
        top = jnp.where(diag_valid, x[:sub], 0.0)
        return top if x.shape[0] == sub else jnp.concatenate([top, x[sub:]], axis=0)

    def sub_block(n, r0, r1, on_diag):
        z = lax.dot_general(q[r0:r1], kbd_ref[n], (((1,), (1,)), ((), ())), preferred_element_type=F32)
        sp = jnp.maximum(z, jnp.log(1.0 + jnp.exp2(jnp.minimum(z, EXP2_CLAMP))) * LOG2E)
        sp = mask_top(sp, on_diag)
        hi = sp.astype(BF16)
        lo = (sp - hi.astype(F32)).astype(BF16)
        c0 = jnp.dot(jnp.concatenate([hi[:, :sub], lo[:, :sub]], axis=1), uu0, preferred_element_type=F32)
        c1 = jnp.dot(jnp.concatenate([hi[:, sub:], lo[:, sub:]], axis=1), uu1, preferred_element_type=F32)
        w = mask_top(jnp.exp2(z - jnp.concatenate([c0[:, :sub], c1[:, :sub]], axis=1)), on_diag)
        pv = jnp.dot(w.astype(BF16), vbd_ref[n], preferred_element_type=F32)
        rn = rn_ref[r0:r1, :]
        acc_ref[r0:r1, :] = acc_ref[r0:r1, :] + pv * jnp.exp2(rn)
        rn_ref[r0:r1, :] = rn - (c0[:, sub:] + c1[:, sub:])

    for hf in range(tiles):
        for sb in reversed(range(nsub)):
            sub_block((i * tiles + hf) * nsub + sb, hf * tq + sb * sub, (hf + 1) * tq, True)

    for hf in range(tiles):
        r0, r1 = hf * tq, (hf + 1) * tq
        first = (i * tiles + hf) * nsub

        def live(r0=r0, r1=r1):
            return (jnp.max(rn_ref[r0:r1, :]) > -SKIP_BITS).astype(jnp.int32)

        def more(carry, first=first):
            t, alive = carry
            return (t < first // walk) & (alive > 0)

        def body(carry, r0=r0, r1=r1, first=first, live=live):
            t, _ = carry
            for k in range(walk):
                sub_block(first - 1 - t * walk - k, r0, r1, False)
            return t + 1, live()

        lax.while_loop(more, body, (jnp.int32(0), live()))
    o_ref[...] = acc_ref[...].astype(BF16)


def _attn(qkv, bsz, s_len):
    width = qkv.shape[1] // 3
    pair = 2 * ATTN_HEAD_DIM
    npair = width // pair
    tq = min(ATTN_TQ, s_len)
    tiles = ATTN_TILES if s_len % (ATTN_TILES * tq) == 0 else 1
    tb = tiles * tq
    nq = s_len // tb
    nsub = tq // ATTN_SUB
    walk = min(ATTN_WALK, tq) // ATTN_SUB
    assert s_len % tb == 0 and tq % ATTN_SUB == 0 and width % pair == 0 and pair == LANES and nsub % walk == 0
    nblk = s_len // ATTN_SUB
    return pl.pallas_call(
        functools.partial(_attn_kernel, tq=tq, tiles=tiles, nsub=nsub, walk=walk),
        grid=(bsz, npair, nq),
        in_specs=[
            pl.BlockSpec((tb, pair), lambda b, p, i: (b * nq + i, p)),
            pl.BlockSpec((s_len, pair), lambda b, p, i: (b, npair + p)),
            pl.BlockSpec((s_len, pair), lambda b, p, i: (b, 2 * npair + p)),
        ],
        out_specs=pl.BlockSpec((tb, pair), lambda b, p, i: (b * nq + i, p)),
        out_shape=jax.ShapeDtypeStruct((bsz * s_len, width), BF16),
        scratch_shapes=[
            pltpu.VMEM((nblk, 2 * ATTN_SUB, pair), BF16),
            pltpu.VMEM((nblk, 2 * ATTN_SUB, pair), BF16),
            pltpu.VMEM((tb, pair), F32),
            pltpu.VMEM((tb, pair), F32),
        ],
        compiler_params=_params(("arbitrary", "arbitrary", "arbitrary")),
        name="attn",
    )(qkv, qkv, qkv)


def _outproj_kernel(h_ref, x_ref, s_ref, a_ref, pw_ref, psc_ref, w_ref, o_ref, buf_ref, *, tiles_per_seq):
    seq_tile = pl.program_id(0) % tiles_per_seq
    _pool_reset(buf_ref, seq_tile)
    r0 = x_ref.shape[1]
    r1 = r0 + s_ref.shape[1]
    acc = jnp.dot(s_ref[...], w_ref[r0:r1, :], preferred_element_type=F32)
    acc = acc + jnp.dot(a_ref[...], w_ref[r1:, :], preferred_element_type=F32)
    pool_o = _pool_tile(x_ref, buf_ref, pw_ref, psc_ref, seq_tile)
    acc = acc + jnp.dot(pool_o.astype(BF16), w_ref[:r0, :], preferred_element_type=F32)
    o_ref[...] = h_ref[...] + acc


def _outproj(h, p, pool_width, ssd_o, attn_o, l, w_pool, pool_scale, w, s_len):
    t, d = h.shape
    tm = min(OUT_TM, s_len)
    assert s_len % tm == 0 and tm >= POOL_HALO and pool_width + ssd_o.shape[1] + attn_o.shape[1] == w.shape[1]

    def tok(width):
        return pl.BlockSpec((tm, width), lambda i: (i, 0))

    return pl.pallas_call(
        functools.partial(_outproj_kernel, tiles_per_seq=s_len // tm),
        grid=(t // tm,),
        in_specs=[tok(d), tok(pool_width), tok(ssd_o.shape[1]), tok(attn_o.shape[1]),
                  _layer_spec(w_pool, l), _layer_spec(pool_scale, l), _layer_spec(w, l)],
        out_specs=tok(d),
        out_shape=jax.ShapeDtypeStruct((t, d), F32),
        scratch_shapes=[pltpu.VMEM((POOL_HALO + tm, pool_width), F32)],
        compiler_params=_params(("arbitrary",)),
        name="outproj",
    )(h, p, ssd_o, attn_o, w_pool, pool_scale, w)


def _rows(v):
    return v[:, None, :]


def _pad_lanes(v):
    return _rows(jnp.pad(v, ((0, 0), (0, LANES - v.shape[1]))))


def kernel(x, ffn1_norm, ffn1_w_gate, ffn1_w_up, ffn1_w_down, mix_norm, w_in, pool_w, pool_scale, conv_w, conv_b, dt_bias, a_log, d_skip, ssd_norm, w_out, ffn2_norm, ffn2_w_gate, ffn2_w_up, ffn2_w_down, final_norm):
    bsz, s_len, d = x.shape
    depth = w_in.shape[0]
    pool_width = pool_scale.shape[1]
    ssd_width = ssd_norm.shape[1]
    heads = dt_bias.shape[1]
    xbc_width = conv_w.shape[2]
    bcw = xbc_width - ssd_width
    attn_width = (w_in.shape[2] - pool_width - ssd_width - xbc_width - heads) // 3
    assert heads <= LANES and heads * SSD_HEAD_DIM == ssd_width and bcw == 2 * SSD_GROUPS * SSD_STATE
    o_z = pool_width
    o_xs = o_z + ssd_width
    o_bc = o_xs + ssd_width
    o_dt = o_bc + bcw
    o_q = o_dt + heads
    slab_width = o_dt + LANES
    assert o_dt % LANES == 0 and pool_width % LANES == 0 and o_q + 3 * attn_width == w_in.shape[2]
    off = {"z": o_z, "xs": o_xs, "bc": o_bc, "dt": o_dt}
    q_scale = ATTN_HEAD_DIM ** -0.5 * LOG2E

    bf = lambda w: w.astype(BF16)
    ffn_g = (_rows(ffn1_norm), _rows(ffn2_norm))
    ffn_w = ((ffn1_w_gate, ffn1_w_up, ffn1_w_down), (ffn2_w_gate, ffn2_w_up, ffn2_w_down))
    mix_g, w_pool, pool_sc, w_o = _rows(mix_norm), bf(pool_w), _rows(pool_scale), bf(w_out)
    ssd_prm = {
        "conv_w": conv_w, "conv_b": _rows(conv_b),
        "dt_bias": _pad_lanes(dt_bias), "dt_bias_t": dt_bias[:, :, None],
        "a_log": _pad_lanes(a_log), "a_log_t": a_log[:, :, None],
        "d_skip": _rows(jnp.repeat(d_skip, SSD_HEAD_DIM, axis=1)), "norm_g": _rows(ssd_norm),
    }

    h = x.reshape(bsz * s_len, d)
    side_cast = _ffn_can_cast(bsz * s_len, d)
    w_cur = tuple(bf(w[0]) for w in ffn_w[0])

    def ffn(h, l, which, w_cur):
        last = l == depth - 1 and which == 1
        nl, nw = (l, 1) if which == 0 else (l + 1, 0)
        nxt = None if last or not side_cast else (nl,) + ffn_w[nw]
        h, w_next = _ffn(h, l, ffn_g[which], *w_cur, final_g=final_norm[None, :] if last else None, nxt=nxt)
        if not last and not side_cast:
            w_next = tuple(bf(w[nl]) for w in ffn_w[nw])
        return h, w_next

    for l in range(depth):
        h, w_cur = ffn(h, l, 0, w_cur)
        p, qkv, dt_t = _inproj(h, l, mix_g, w_in, slab_width, heads, q_scale)
        ssd_o = _ssd(p, dt_t, off, l, ssd_prm, bsz, s_len)
        attn_o = _attn(qkv, bsz, s_len)
        h = _outproj(h, p, pool_width, ssd_o, attn_o, l, w_pool, pool_sc, w_o, s_len)
        h, w_cur = ffn(h, l, 1, w_cur)
    return h.reshape(bsz, s_len, d)
```
